```python
import jax, jax.numpy as jnp
from jax import lax
import numpy as np

D_MODEL = 1024
BATCH = 32
SEQ = 2048
DEPTH = 2
DEC_BATCH = 128
DEC_SEQ = 1
PAST_LEN = 16384
PAGE_SIZE = 128

N_A_LAYERS = (DEPTH + 1) // 2
N_B_LAYERS = DEPTH // 2
WINDOWS = (128, 512, 2048)
DILATIONS = (1, 4, 16)
N_GROUPS = 3
H_A = 8
DH_A = 64
N_BUCKETS = 32
T5_MAX_DISTANCE = 2048
H_B = 16
Q_LORA = 384
KV_LORA = 256
QK_NOPE = 64
QK_ROPE = 32
V_HEAD = 64
ROPE_BASE = 10000.0
Q_BLOCK = 128
D_FF = 2816
CONV_W = 3
EPS = 1e-6
NEG = -1e30

kernel_name = "hybrid_dilated_mla_convffn_step"


def _rmsnorm(x, g):
    x32 = x.astype(jnp.float32)
    y = x32 * lax.rsqrt(jnp.mean(x32 * x32, axis=-1, keepdims=True) + EPS)
    return (y * g.astype(jnp.float32)).astype(x.dtype)


def _t5_bucket(dist):
    n = np.asarray(dist)
    max_exact = N_BUCKETS // 2
    large = max_exact + (np.log(np.maximum(n, 1) / max_exact) / np.log(T5_MAX_DISTANCE / max_exact)
                         * (N_BUCKETS - max_exact)).astype(np.int32)
    large = np.minimum(large, N_BUCKETS - 1)
    return np.where(n < max_exact, n, large).astype(np.int32)


def _dsw_biases(table):
    out = []
    for g, (w, d) in enumerate(zip(WINDOWS, DILATIONS)):
        buckets = _t5_bucket(np.arange(w // d + 1) * d)
        out.append(table[buckets][:, g * H_A:(g + 1) * H_A].T)
    return out


def _dsw_prompt_group(q, k, v, bias_g, window, dil):
    B, S, H, E = q.shape
    blk = window // dil
    span = dil * blk
    s_pad = -(-S // span) * span
    L = s_pad // dil
    nb = L // blk

    def to_blocks(t):
        t = jnp.pad(t, ((0, 0), (0, s_pad - S), (0, 0), (0, 0)))
        t = t.reshape(B, L, dil, H, E).transpose(0, 2, 1, 3, 4)
        return t.reshape(B, dil, nb, blk, H, E)

    def with_prev(t):
        prev = jnp.pad(t[:, :, :-1], ((0, 0), (0, 0), (1, 0), (0, 0), (0, 0), (0, 0)))
        return jnp.concatenate([prev, t], axis=3)

    qb = to_blocks(q)
    kc = with_prev(to_blocks(k))
    vc = with_prev(to_blocks(v))
    rel = np.arange(blk)[:, None] + blk - np.arange(2 * blk)[None, :]
    band = (rel >= 0) & (rel <= blk)
    key_idx = np.arange(nb)[:, None] * blk - blk + np.arange(2 * blk)[None, :]
    mask = band[None, :, None, :] & (key_idx >= 0)[:, None, None, :]
    bias = bias_g[:, np.clip(rel, 0, blk)].transpose(1, 0, 2).astype(jnp.float32)
    s = jnp.einsum('bdnqhe,bdnkhe->bdnqhk', qb, kc, preferred_element_type=jnp.float32) * (DH_A ** -0.5) + bias
    s = jnp.where(mask, s, NEG)
    m = jnp.max(s, axis=-1, keepdims=True)
    p = jnp.exp(s - m)
    l = jnp.sum(p, axis=-1)
    o = jnp.einsum('bdnqhk,bdnkhe->bdnqhe', p, vc) / l[..., None]
    lse = m[..., 0] + jnp.log(l)
    o = o.reshape(B, dil, L, H, E).transpose(0, 2, 1, 3, 4).reshape(B, s_pad, H, E)[:, :S]
    lse = lse.reshape(B, dil, L, H).transpose(0, 2, 1, 3).reshape(B, s_pad, H)[:, :S]
    return o, lse


def _dsw_sample_group(q, k, v, buf, bias_g, window, dil):
    T = q.shape[1]
    Wb = buf.shape[1]
    ext = jnp.concatenate([buf.astype(k.dtype), jnp.stack([k, v], axis=2)], axis=1)
    idx = Wb + np.arange(T)[:, None] - np.arange(window // dil + 1)[None, :] * dil
    valid = idx >= 0
    gk = ext[:, np.clip(idx, 0, None)]
    s = jnp.einsum('bthe,btjhe->bthj', q, gk[:, :, :, 0], preferred_element_type=jnp.float32) * (DH_A ** -0.5)
    s = jnp.where(valid[:, None, :], s + bias_g.astype(jnp.float32), NEG)
    m = jnp.max(s, axis=-1, keepdims=True)
    p = jnp.exp(s - m)
    l = jnp.sum(p, axis=-1)
    o = jnp.einsum('bthj,btjhe->bthe', p, gk[:, :, :, 1]) / l[..., None]
    lse = m[..., 0] + jnp.log(l)
    new_len = min(window, Wb + T)
    return o, lse, ext[:, Wb + T - new_len:]


def _dsw_mixer(h, w_qkv, w_o, biases, bufs):
    B, S, _ = h.shape
    qkv = (h @ w_qkv).reshape(B, S, N_GROUPS, 3, H_A, DH_A)
    outs, lses, new_bufs = [], [], []
    for g in range(N_GROUPS):
        q, k, v = qkv[:, :, g, 0], qkv[:, :, g, 1], qkv[:, :, g, 2]
        if bufs is None:
            o, lse = _dsw_prompt_group(q, k, v, biases[g], WINDOWS[g], DILATIONS[g])
            keep = min(WINDOWS[g], S)
            nbuf = jnp.stack([k, v], axis=2)[:, S - keep:]
        else:
            o, lse, nbuf = _dsw_sample_group(q, k, v, bufs[g], biases[g], WINDOWS[g], DILATIONS[g])
        outs.append(o)
        lses.append(lse)
        new_bufs.append(nbuf)
    wts = jax.nn.softmax(jnp.stack(lses, axis=2), axis=2)
    o = jnp.einsum('bsgh,bsghe->bshe', wts, jnp.stack(outs, axis=2))
    return o.astype(h.dtype).reshape(B, S, H_A * DH_A) @ w_o, new_bufs


def _rope_tables(pos):
    inv = jnp.asarray(ROPE_BASE ** (-np.arange(0, QK_ROPE, 2) / QK_ROPE), dtype=jnp.float32)
    ang = pos.astype(jnp.float32)[:, None] * inv[None, :]
    return jnp.cos(ang), jnp.sin(ang)


def _rope(x, cos, sin):
    x32 = x.astype(jnp.float32)
    half = QK_ROPE // 2
    x1, x2 = x32[..., :half], x32[..., half:]
    return jnp.concatenate([x1 * cos - x2 * sin, x2 * cos + x1 * sin], axis=-1).astype(x.dtype)


def _mla_prompt_attn(q_nope, q_rope, c_kv, k_r, w_uk, w_uv):
    B, S = q_nope.shape[:2]
    k_nope = jnp.einsum('bsc,chn->bshn', c_kv, w_uk)
    v = jnp.einsum('bsc,chv->bshv', c_kv, w_uv)
    nq = S // Q_BLOCK
    qn = q_nope.reshape(B, nq, Q_BLOCK, H_B, QK_NOPE).swapaxes(0, 1)
    qr = q_rope.reshape(B, nq, Q_BLOCK, H_B, QK_ROPE).swapaxes(0, 1)
    starts = jnp.arange(nq) * Q_BLOCK
    kpos = jnp.arange(S)
    scale = (QK_NOPE + QK_ROPE) ** -0.5

    def block(args):
        qn_b, qr_b, st = args
        s = (jnp.einsum('bqhn,bkhn->bhqk', qn_b, k_nope, preferred_element_type=jnp.float32)
             + jnp.einsum('bqhr,bkr->bhqk', qr_b, k_r, preferred_element_type=jnp.float32)) * scale
        mask = kpos[None, :] <= (st + jnp.arange(Q_BLOCK))[:, None]
        p = jax.nn.softmax(jnp.where(mask, s, NEG), axis=-1)
        return jnp.einsum('bhqk,bkhv->bqhv', p, v).astype(v.dtype)

    o = lax.map(block, (qn, qr, starts))
    return o.swapaxes(0, 1).reshape(B, S, H_B, V_HEAD)


def _mla_sample_attn(q_nope, q_rope, c_kv, k_r, w_uk, w_uv, ckv_pool, kr_pool, page_table):
    DB, T = q_nope.shape[:2]
    past = page_table.shape[1] * ckv_pool.shape[1]
    ckv_past = ckv_pool[page_table].reshape(DB, past, KV_LORA)
    kr_past = kr_pool[page_table].reshape(DB, past, QK_ROPE)
    scale = (QK_NOPE + QK_ROPE) ** -0.5
    q_lat = jnp.einsum('bthn,chn->bthc', q_nope, w_uk)
    s_past = (jnp.einsum('bthc,bkc->bhtk', q_lat, ckv_past, preferred_element_type=jnp.float32)
              + jnp.einsum('bthr,bkr->bhtk', q_rope, kr_past, preferred_element_type=jnp.float32)) * scale
    s_new = (jnp.einsum('bthc,bkc->bhtk', q_lat, c_kv, preferred_element_type=jnp.float32)
             + jnp.einsum('bthr,bkr->bhtk', q_rope, k_r, preferred_element_type=jnp.float32)) * scale
    s_new = jnp.where(np.tril(np.ones((T, T), dtype=bool)), s_new, NEG)
    p = jax.nn.softmax(jnp.concatenate([s_past, s_new], axis=-1), axis=-1)
    o_lat = (jnp.einsum('bhtk,bkc->bthc', p[..., :past], ckv_past)
             + jnp.einsum('bhtk,bkc->bthc', p[..., past:], c_kv))
    return jnp.einsum('bthc,chv->bthv', o_lat, w_uv).astype(q_nope.dtype)


def _mla_mixer(h, cos, sin, w_in, g_q, g_kv, w_q, w_kv, w_o, past):
    B, S, _ = h.shape
    proj = h @ w_in
    c_q = _rmsnorm(proj[..., :Q_LORA], g_q)
    c_kv = _rmsnorm(proj[..., Q_LORA:Q_LORA + KV_LORA], g_kv)
    k_r = _rope(proj[..., Q_LORA + KV_LORA:], cos, sin)
    q = (c_q @ w_q).reshape(B, S, H_B, QK_NOPE + QK_ROPE)
    q_nope = q[..., :QK_NOPE]
    q_rope = _rope(q[..., QK_NOPE:], cos[:, None, :], sin[:, None, :])
    w_kv3 = w_kv.reshape(KV_LORA, H_B, QK_NOPE + V_HEAD)
    w_uk, w_uv = w_kv3[..., :QK_NOPE], w_kv3[..., QK_NOPE:]
    if past is None:
        o = _mla_prompt_attn(q_nope, q_rope, c_kv, k_r, w_uk, w_uv)
    else:
        o = _mla_sample_attn(q_nope, q_rope, c_kv, k_r, w_uk, w_uv, *past)
    return o.reshape(B, S, H_B * V_HEAD) @ w_o, c_kv, k_r


def _conv_ffn(h, state, w_up, conv_w, conv_b, w_down):
    S = h.shape[1]
    u = h @ w_up
    if state is None:
        ext = jnp.pad(u, ((0, 0), (CONV_W - 1, 0), (0, 0)))
    else:
        ext = jnp.concatenate([state.astype(u.dtype), u], axis=1)
    y = conv_b
    for j in range(CONV_W):
        y = y + conv_w[j] * ext[:, j:j + S]
    gate, val = y[..., :D_FF], y[..., D_FF:]
    return (jax.nn.silu(gate) * val) @ w_down, ext[:, -(CONV_W - 1):]


def setup_inputs(seed: int = 0) -> dict:
    key = jax.random.key(seed)
    ks = iter(jax.random.split(key, 32))

    def nrm(shape, scale):
        return jax.random.normal(next(ks), shape, jnp.float32) * scale

    n_pages = PAST_LEN // PAGE_SIZE
    n_phys = (DEC_BATCH * n_pages * 5) // 4
    page_table = jax.random.permutation(next(ks), n_phys)[:DEC_BATCH * n_pages].reshape(DEC_BATCH, n_pages).astype(jnp.int32)
    d_att = H_A * DH_A
    return {
        "x_prompt": nrm((BATCH, SEQ, D_MODEL), 1.0),
        "x_sample": nrm((DEC_BATCH, DEC_SEQ, D_MODEL), 1.0),
        "cache_dsw_g0": nrm((N_A_LAYERS, DEC_BATCH, min(WINDOWS[0], PAST_LEN), 2, H_A, DH_A), 1.0),
        "cache_dsw_g1": nrm((N_A_LAYERS, DEC_BATCH, min(WINDOWS[1], PAST_LEN), 2, H_A, DH_A), 1.0),
        "cache_dsw_g2": nrm((N_A_LAYERS, DEC_BATCH, min(WINDOWS[2], PAST_LEN), 2, H_A, DH_A), 1.0),
        "cache_mla_ckv": nrm((N_B_LAYERS, n_phys, PAGE_SIZE, KV_LORA), 1.0),
        "cache_mla_kr": nrm((N_B_LAYERS, n_phys, PAGE_SIZE, QK_ROPE), 1.0),
        "state_ffn_conv": nrm((DEPTH, DEC_BATCH, CONV_W - 1, 2 * D_FF), 1.0),
        "page_table": page_table,
        "rel_bias_table": nrm((N_BUCKETS, N_GROUPS * H_A), 0.5),
        "norm_mix": 1.0 + nrm((DEPTH, D_MODEL), 0.02),
        "norm_ffn": 1.0 + nrm((DEPTH, D_MODEL), 0.02),
        "norm_final": 1.0 + nrm((D_MODEL,), 0.02),
        "w_qkv_dsw": nrm((N_A_LAYERS, D_MODEL, N_GROUPS * 3 * d_att), D_MODEL ** -0.5),
        "w_o_dsw": nrm((N_A_LAYERS, d_att, D_MODEL), d_att ** -0.5),
        "w_in_mla": nrm((N_B_LAYERS, D_MODEL, Q_LORA + KV_LORA + QK_ROPE), D_MODEL ** -0.5),
        "g_q_mla": 1.0 + nrm((N_B_LAYERS, Q_LORA), 0.02),
        "g_kv_mla": 1.0 + nrm((N_B_LAYERS, KV_LORA), 0.02),
        "w_q_mla": nrm((N_B_LAYERS, Q_LORA, H_B * (QK_NOPE + QK_ROPE)), Q_LORA ** -0.5),
        "w_kv_mla": nrm((N_B_LAYERS, KV_LORA, H_B * (QK_NOPE + V_HEAD)), KV_LORA ** -0.5),
        "w_o_mla": nrm((N_B_LAYERS, H_B * V_HEAD, D_MODEL), (H_B * V_HEAD) ** -0.5),
        "w_up_ffn": nrm((DEPTH, D_MODEL, 2 * D_FF), D_MODEL ** -0.5),
        "conv_w_ffn": nrm((DEPTH, CONV_W, 2 * D_FF), CONV_W ** -0.5),
        "conv_b_ffn": nrm((DEPTH, 2 * D_FF), 0.02),
        "w_down_ffn": nrm((DEPTH, D_FF, D_MODEL), D_FF ** -0.5),
    }


def reference(x_prompt, x_sample, cache_dsw_g0, cache_dsw_g1, cache_dsw_g2, cache_mla_ckv, cache_mla_kr,
              state_ffn_conv, page_table, rel_bias_table, norm_mix, norm_ffn, norm_final, w_qkv_dsw, w_o_dsw,
              w_in_mla, g_q_mla, g_kv_mla, w_q_mla, w_kv_mla, w_o_mla, w_up_ffn, conv_w_ffn, conv_b_ffn, w_down_ffn):
    biases = _dsw_biases(rel_bias_table)
    S = x_prompt.shape[1]
    T = x_sample.shape[1]
    past_len = page_table.shape[1] * cache_mla_ckv.shape[2]
    cos_p, sin_p = _rope_tables(jnp.arange(S))
    cos_s, sin_s = _rope_tables(past_len + jnp.arange(T))
    dsw_caches = (cache_dsw_g0, cache_dsw_g1, cache_dsw_g2)
    dsw_new_p = [[] for _ in range(N_GROUPS)]
    dsw_new_s = [[] for _ in range(N_GROUPS)]
    ckv_p, ckv_s, kr_p, kr_s, conv_p, conv_s = [], [], [], [], [], []
    xp, xs = x_prompt, x_sample
    for i in range(DEPTH):
        hp = _rmsnorm(xp, norm_mix[i])
        hs = _rmsnorm(xs, norm_mix[i])
        if i % 2 == 0:
            a = i // 2
            mp, bufs_p = _dsw_mixer(hp, w_qkv_dsw[a], w_o_dsw[a], biases, None)
            ms, bufs_s = _dsw_mixer(hs, w_qkv_dsw[a], w_o_dsw[a], biases, [c[a] for c in dsw_caches])
            for g in range(N_GROUPS):
                dsw_new_p[g].append(bufs_p[g])
                dsw_new_s[g].append(bufs_s[g])
        else:
            b = i // 2
            mla_w = (w_in_mla[b], g_q_mla[b], g_kv_mla[b], w_q_mla[b], w_kv_mla[b], w_o_mla[b])
            mp, ckv_new_p, kr_new_p = _mla_mixer(hp, cos_p, sin_p, *mla_w, None)
            ms, ckv_new_s, kr_new_s = _mla_mixer(hs, cos_s, sin_s, *mla_w,
                                                 (cache_mla_ckv[b], cache_mla_kr[b], page_table))
            ckv_p.append(ckv_new_p)
            kr_p.append(kr_new_p)
            ckv_s.append(ckv_new_s)
            kr_s.append(kr_new_s)
        xp = xp + mp
        xs = xs + ms
        ffn_w = (w_up_ffn[i], conv_w_ffn[i], conv_b_ffn[i], w_down_ffn[i])
        fp, cp = _conv_ffn(_rmsnorm(xp, norm_ffn[i]), None, *ffn_w)
        fs, cs = _conv_ffn(_rmsnorm(xs, norm_ffn[i]), state_ffn_conv[i], *ffn_w)
        conv_p.append(cp)
        conv_s.append(cs)
        xp = xp + fp
        xs = xs + fs
    y_prompt = _rmsnorm(xp, norm_final)
    y_sample = _rmsnorm(xs, norm_final)
    new_dsw_g0_prompt = jnp.stack(dsw_new_p[0])
    new_dsw_g0_sample = jnp.stack(dsw_new_s[0])
    new_dsw_g1_prompt = jnp.stack(dsw_new_p[1])
    new_dsw_g1_sample = jnp.stack(dsw_new_s[1])
    new_dsw_g2_prompt = jnp.stack(dsw_new_p[2])
    new_dsw_g2_sample = jnp.stack(dsw_new_s[2])
    new_mla_ckv_prompt = jnp.stack(ckv_p)
    new_mla_ckv_sample = jnp.stack(ckv_s)
    new_mla_kr_prompt = jnp.stack(kr_p)
    new_mla_kr_sample = jnp.stack(kr_s)
    new_ffn_conv_prompt = jnp.stack(conv_p)
    new_ffn_conv_sample = jnp.stack(conv_s)
    return (y_prompt, y_sample, new_dsw_g0_prompt, new_dsw_g0_sample, new_dsw_g1_prompt, new_dsw_g1_sample,
            new_dsw_g2_prompt, new_dsw_g2_sample, new_mla_ckv_prompt, new_mla_ckv_sample,
            new_mla_kr_prompt, new_mla_kr_sample, new_ffn_conv_prompt, new_ffn_conv_sample)
```

```python
import functools

import numpy as np
import jax
import jax.numpy as jnp
from jax import lax
from jax.experimental import pallas as pl
from jax.experimental.pallas import tpu as pltpu

WINDOWS = (128, 512, 2048)
DILATIONS = (1, 4, 16)
N_GROUPS = 3
H_A = 8
DH_A = 64
D_ATT = H_A * DH_A
BLK = 128
N_BUCKETS = 32
T5_MAX_DISTANCE = 2048
H_B = 16
Q_LORA = 384
KV_LORA = 256
QK_NOPE = 64
QK_ROPE = 32
V_HEAD = 64
ROPE_BASE = 10000.0
CONV_W = 3
EPS = 1e-6
NEG = -1e30

LANES = 128
VMEM_LIMIT = 56 * 1024 * 1024

_BF = jnp.bfloat16
_F32 = jnp.float32


def _params(n_axes, vmem=VMEM_LIMIT):
    return pltpu.CompilerParams(dimension_semantics=("arbitrary",) * n_axes, vmem_limit_bytes=vmem)


def _const_spec(shape):
    nd = len(shape)
    return pl.BlockSpec(shape, lambda *_: (0,) * nd, pipeline_mode=pl.Buffered(1))


def _rms(x, g):
    return x * lax.rsqrt(jnp.mean(x * x, axis=-1, keepdims=True) + EPS) * g


def _linear_kernel(*refs, has_norm, has_res, n_chunk):
    it = iter(refs)
    x_ref = next(it)
    g_ref = next(it) if has_norm else None
    w_ref = next(it)
    r_ref = next(it) if has_res else None
    o_ref = next(it)
    x = x_ref[...]
    if has_norm:
        x = _rms(x.astype(_F32), g_ref[...])
    xb = x.astype(_BF)
    n = w_ref.shape[1]
    for c0 in range(0, n, n_chunk):
        y = jnp.dot(xb, w_ref[:, c0:c0 + n_chunk], preferred_element_type=_F32)
        if has_res:
            y = y + r_ref[:, c0:c0 + n_chunk]
        o_ref[:, c0:c0 + n_chunk] = y.astype(o_ref.dtype)


def _linear(x, w, *, norm_g=None, residual=None, out_dtype=_F32, tm=512, name="linear"):
    m, k = x.shape
    n = w.shape[1]
    tm = min(tm, m)
    assert m % tm == 0
    n_chunk = 512 if n % 512 == 0 else n
    args, specs = [x], [pl.BlockSpec((tm, k), lambda i: (i, 0))]
    if norm_g is not None:
        args.append(norm_g.reshape(1, k))
        specs.append(_const_spec((1, k)))
    args.append(w)
    specs.append(_const_spec((k, n)))
    if residual is not None:
        args.append(residual)
        specs.append(pl.BlockSpec((tm, n), lambda i: (i, 0)))
    return pl.pallas_call(
        functools.partial(_linear_kernel, has_norm=norm_g is not None, has_res=residual is not None,
                          n_chunk=n_chunk),
        grid=(m // tm,),
        in_specs=specs,
        out_specs=pl.BlockSpec((tm, n), lambda i: (i, 0)),
        out_shape=jax.ShapeDtypeStruct((m, n), out_dtype),
        compiler_params=_params(1),
        name=name,
    )(*args)


def _qkv_prompt_kernel(x_ref, g_ref, w_ref, qkv_ref, c0_ref, c1_ref, c2_ref, *, tm, tiles_per_seq, keeps):
    i = pl.program_id(0)
    last_tile = (i % tiles_per_seq) == tiles_per_seq - 1
    hb = _rms(x_ref[...], g_ref[...]).astype(_BF)
    cache_refs = (c0_ref, c1_ref, c2_ref)
    for g in range(N_GROUPS):
        base = g * 3 * D_ATT
        q = jnp.dot(hb, w_ref[:, base:base + D_ATT], preferred_element_type=_F32)
        qkv_ref[:, base:base + D_ATT] = (q * (DH_A ** -0.5)).astype(_BF)
        for part in (1, 2):
            c0 = base + part * D_ATT
            y = jnp.dot(hb, w_ref[:, c0:c0 + D_ATT], preferred_element_type=_F32)
            qkv_ref[:, c0:c0 + D_ATT] = y.astype(_BF)
            cref = cache_refs[g]
            dst = slice((part - 1) * D_ATT, part * D_ATT)
            keep = keeps[g]
            if keep == tm * tiles_per_seq:
                cref[:, dst] = y
            else:
                @pl.when(last_tile)
                def _(cref=cref, y=y, dst=dst, keep=keep):
                    cref[:, dst] = y[tm - keep:, :]


def _qkv_prompt(x2d, g, w, batch, seq, tm=512):
    m, d = x2d.shape
    n = w.shape[1]
    tps = seq // tm
    keeps = tuple(min(wd, seq) for wd in WINDOWS)
    cache_shapes, cache_specs = [], []
    for keep in keeps:
        assert keep == seq or keep <= tm
        cache_shapes.append(jax.ShapeDtypeStruct((batch * keep, 2 * D_ATT), _F32))
        if keep == seq:
            cache_specs.append(pl.BlockSpec((tm, 2 * D_ATT), lambda i: (i, 0)))
        else:
            cache_specs.append(pl.BlockSpec((keep, 2 * D_ATT), lambda i: (i // tps, 0)))
    return pl.pallas_call(
        functools.partial(_qkv_prompt_kernel, tm=tm, tiles_per_seq=tps, keeps=keeps),
        grid=(m // tm,),
        in_specs=[pl.BlockSpec((tm, d), lambda i: (i, 0)), _const_spec((1, d)), _const_spec((d, n))],
        out_specs=[pl.BlockSpec((tm, n), lambda i: (i, 0))] + cache_specs,
        out_shape=[jax.ShapeDtypeStruct((m, n), _BF)] + cache_shapes,
        compiler_params=_params(1),
        name="dsw_qkv_prompt",
    )(x2d, g.reshape(1, d), w)


def _dsw_prompt_kernel(q_ref, kp_ref, kc_ref, vp_ref, vc_ref, bias_ref, o_ref, lse_ref):
    n = pl.program_id(2)
    qi = lax.broadcasted_iota(jnp.int32, (BLK, 2 * BLK), 0)
    ki = lax.broadcasted_iota(jnp.int32, (BLK, 2 * BLK), 1)
    rel = qi + BLK - ki
    has_prev = jnp.logical_or(ki >= BLK, n > 0)
    valid = (rel >= 0) & (rel <= BLK) & has_prev
    lo = lax.broadcasted_iota(jnp.int32, (BLK, LANES), 1) < DH_A
    for pair in range(H_A // 2):
        sl = slice(pair * LANES, (pair + 1) * LANES)
        qp = q_ref[:, sl]
        kp = jnp.concatenate([kp_ref[:, sl], kc_ref[:, sl]], axis=0)
        vp = jnp.concatenate([vp_ref[:, sl], vc_ref[:, sl]], axis=0)
        outs, lses = [], []
        for hh in range(2):
            sel = lo if hh == 0 else jnp.logical_not(lo)
            qm = jnp.where(sel, qp, jnp.zeros_like(qp))
            s = lax.dot_general(qm, kp, (((1,), (1,)), ((), ())), preferred_element_type=_F32)
            s = jnp.where(valid, s + bias_ref[pair * 2 + hh], NEG)
            m = jnp.max(s, axis=-1, keepdims=True)
            p = jnp.exp(s - m)
            l = jnp.sum(p, axis=-1, keepdims=True)
            pv = jnp.dot(p.astype(_BF), vp, preferred_element_type=_F32)
            outs.append(pv / l)
            lses.append(m + jnp.log(l))
        o_ref[:, sl] = jnp.where(lo, outs[0], outs[1])
        lse_ref[:, sl] = jnp.where(lo, jnp.broadcast_to(lses[0], (BLK, LANES)),
                                   jnp.broadcast_to(lses[1], (BLK, LANES)))


def _dsw_prompt_attn(qkv, bias_full, g, batch, seq):
    d = DILATIONS[g]
    assert seq % WINDOWS[g] == 0
    sub = seq // d
    nb = sub // BLK
    width = qkv.shape[-1]
    ncol = width // D_ATT
    qv = qkv.reshape(batch, sub, d * width)

    def col(part):
        return lambda b, r, n: (b, n, r * ncol + g * 3 + part)

    def col_prev(part):
        return lambda b, r, n: (b, jnp.maximum(n - 1, 0), r * ncol + g * 3 + part)

    blk = (None, BLK, D_ATT)
    o, lse = pl.pallas_call(
        _dsw_prompt_kernel,
        grid=(batch, d, nb),
        in_specs=[pl.BlockSpec(blk, col(0)), pl.BlockSpec(blk, col_prev(1)), pl.BlockSpec(blk, col(1)),
                  pl.BlockSpec(blk, col_prev(2)), pl.BlockSpec(blk, col(2)),
                  _const_spec((H_A, BLK, 2 * BLK))],
        out_specs=[pl.BlockSpec(blk, lambda b, r, n: (b, n, r)), pl.BlockSpec(blk, lambda b, r, n: (b, n, r))],
        out_shape=[jax.ShapeDtypeStruct((batch, sub, d * D_ATT), _F32)] * 2,
        compiler_params=_params(3),
        name=f"dsw_prompt_attn_g{g}",
    )(qv, qv, qv, qv, qv, bias_full)
    return o.reshape(batch * seq, D_ATT), lse.reshape(batch * seq, D_ATT)


def _dsw_merge_kernel(o0_ref, o1_ref, o2_ref, l0_ref, l1_ref, l2_ref, w_ref, x_ref, out_ref):
    la, lb, lc = l0_ref[...], l1_ref[...], l2_ref[...]
    mx = jnp.maximum(jnp.maximum(la, lb), lc)
    ea, eb, ec = jnp.exp(la - mx), jnp.exp(lb - mx), jnp.exp(lc - mx)
    o = (ea * o0_ref[...] + eb * o1_ref[...] + ec * o2_ref[...]) / (ea + eb + ec)
    out_ref[...] = x_ref[...] + jnp.dot(o.astype(_BF), w_ref[...], preferred_element_type=_F32)


def _dsw_merge(outs, lses, w_o, x2d, tm=512):
    m, d = x2d.shape
    tm = min(tm, m)
    spec = pl.BlockSpec((tm, D_ATT), lambda i: (i, 0))
    return pl.pallas_call(
        _dsw_merge_kernel,
        grid=(m // tm,),
        in_specs=[spec] * 6 + [_const_spec((D_ATT, d)), pl.BlockSpec((tm, d), lambda i: (i, 0))],
        out_specs=pl.BlockSpec((tm, d), lambda i: (i, 0)),
        out_shape=jax.ShapeDtypeStruct((m, d), _F32),
        compiler_params=_params(1),
        name="dsw_merge_out",
    )(*outs, *lses, w_o, x2d)


_KV_SLABS = 2 * D_ATT // LANES
_PAIRS = H_A // 2


def _dsw_sample_kernel(qkv_ref, cache_ref, brow_ref, b0_ref, o_ref, lse_ref, new_ref, *, g, window, dil,
                       copy_rows):
    base = g * 3 * _PAIRS
    head = lax.broadcasted_iota(jnp.int32, (H_A, LANES), 0)
    lane_head = lax.broadcasted_iota(jnp.int32, (H_A, LANES), 1) // DH_A
    stride = _KV_SLABS * dil
    s = brow_ref[...]
    s_new = b0_ref[...]
    hms = []
    for p in range(_PAIRS):
        hm = head == lane_head + 2 * p
        hms.append(hm)
        q = qkv_ref[base + p:base + p + 1, :] * (DH_A ** -0.5)
        qrows = jnp.where(hm, jnp.broadcast_to(q, (H_A, LANES)), 0.0).astype(_BF)
        kc = cache_ref[pl.ds(p, BLK, stride=stride), :].astype(_BF)
        s = s + lax.dot_general(qrows, kc, (((1,), (1,)), ((), ())), preferred_element_type=_F32)
        knew = qkv_ref[base + _PAIRS + p:base + _PAIRS + p + 1, :]
        prod = q.astype(_BF).astype(_F32) * knew.astype(_BF).astype(_F32)
        s_new = s_new + jnp.sum(jnp.where(hm, jnp.broadcast_to(prod, (H_A, LANES)), 0.0), axis=-1, keepdims=True)
    m = jnp.maximum(jnp.max(s, axis=-1, keepdims=True), s_new)
    pr = jnp.exp(s - m)
    p_new = jnp.exp(s_new - m)
    l = jnp.sum(pr, axis=-1, keepdims=True) + p_new
    lse8 = jnp.broadcast_to(m + jnp.log(l), (H_A, LANES))
    prb = pr.astype(_BF)
    pnb = p_new.astype(_BF).astype(_F32)
    for p in range(_PAIRS):
        vc = cache_ref[pl.ds(_PAIRS + p, BLK, stride=stride), :].astype(_BF)
        vnew = qkv_ref[base + 2 * _PAIRS + p:base + 2 * _PAIRS + p + 1, :]
        o8 = jnp.dot(prb, vc, preferred_element_type=_F32) + pnb * vnew.astype(_BF).astype(_F32)
        o8 = o8 / l
        o_ref[p:p + 1, :] = jnp.sum(jnp.where(hms[p], o8, 0.0), axis=0, keepdims=True)
        lse_ref[p:p + 1, :] = jnp.sum(jnp.where(hms[p], lse8, 0.0), axis=0, keepdims=True)
    total = (window - 1) * _KV_SLABS
    for r0 in range(0, total, copy_rows):
        nr = min(copy_rows, total - r0)
        new_ref[r0:r0 + nr, :] = cache_ref[r0 + _KV_SLABS:r0 + _KV_SLABS + nr, :]
    new_ref[total:total + _KV_SLABS, :] = qkv_ref[base + _PAIRS:base + 3 * _PAIRS, :]


def _dsw_sample(qkv_s, cache, bias_g, g):
    db = qkv_s.shape[0]
    window, dil = WINDOWS[g], DILATIONS[g]
    assert cache.shape[1] == window
    slabs = qkv_s.shape[-1] // LANES
    cache2 = cache.reshape(db, window * _KV_SLABS, LANES)
    brow = bias_g[:, BLK:0:-1]
    b0 = bias_g[:, 0:1]
    o, lse, new = pl.pallas_call(
        functools.partial(_dsw_sample_kernel, g=g, window=window, dil=dil, copy_rows=512),
        grid=(db,),
        in_specs=[pl.BlockSpec((None, slabs, LANES), lambda b: (b, 0, 0)),
                  pl.BlockSpec((None, window * _KV_SLABS, LANES), lambda b: (b, 0, 0)),
                  _const_spec((H_A, BLK)), _const_spec((H_A, 1))],
        out_specs=[pl.BlockSpec((None, _PAIRS, LANES), lambda b: (b, 0, 0)),
                   pl.BlockSpec((None, _PAIRS, LANES), lambda b: (b, 0, 0)),
                   pl.BlockSpec((None, window * _KV_SLABS, LANES), lambda b: (b, 0, 0))],
        out_shape=[jax.ShapeDtypeStruct((db, _PAIRS, LANES), _F32), jax.ShapeDtypeStruct((db, _PAIRS, LANES), _F32),
                   jax.ShapeDtypeStruct((db, window * _KV_SLABS, LANES), _F32)],
        compiler_params=_params(1),
        name=f"dsw_sample_g{g}",
    )(qkv_s.reshape(db, slabs, LANES), cache2, brow, b0)
    return o.reshape(db, D_ATT), lse.reshape(db, D_ATT), new


def _ffn_gate(ug, ug1, ug2, uv, uv1, uv2, cw_ref, cb_ref, cg, cv, chunk):
    def conv(u, u1, u2, c0):
        cs = slice(c0, c0 + chunk)
        return cb_ref[:, cs] + cw_ref[0:1, cs] * u2 + cw_ref[1:2, cs] * u1 + cw_ref[2:3, cs] * u
    yg = conv(ug, ug1, ug2, cg)
    yv = conv(uv, uv1, uv2, cv)
    return (yg * (1.0 / (1.0 + jnp.exp(-yg))) * yv).astype(_BF)


def _ffn_prompt_kernel(*refs, ts, d_ff, chunk, final_norm):
    if final_norm:
        x_ref, st_ref, g_ref, wup_ref, cw_ref, cb_ref, wdn_ref, gf_ref, o_ref, ns_ref, carry_ref, act_ref = refs
    else:
        x_ref, st_ref, g_ref, wup_ref, cw_ref, cb_ref, wdn_ref, o_ref, ns_ref, carry_ref, act_ref = refs
        gf_ref = None
    s = pl.program_id(1)

    @pl.when(s == 0)
    def _():
        carry_ref[...] = st_ref[...]

    x = x_ref[...]
    hb = _rms(x, g_ref[...]).astype(_BF)
    row = lax.broadcasted_iota(jnp.int32, (ts, chunk), 0)
    first, second = row == 0, row == 1

    def up(c0):
        u = jnp.dot(hb, wup_ref[:, c0:c0 + chunk], preferred_element_type=_F32)
        p2 = carry_ref[0:1, c0:c0 + chunk]
        p1 = carry_ref[1:2, c0:c0 + chunk]
        u1 = jnp.where(first, p1, pltpu.roll(u, 1, 0))
        u2 = jnp.where(first, p2, jnp.where(second, p1, pltpu.roll(u, 2, 0)))
        carry_ref[:, c0:c0 + chunk] = u[ts - 2:ts, :]
        return u, u1, u2

    for c in range(d_ff // chunk):
        cg, cv = c * chunk, d_ff + c * chunk
        act_ref[:, cg:cg + chunk] = _ffn_gate(*up(cg), *up(cv), cw_ref, cb_ref, cg, cv, chunk)
    y = x + jnp.dot(act_ref[...], wdn_ref[...], preferred_element_type=_F32)
    if final_norm:
        y = _rms(y, gf_ref[...])
    o_ref[...] = y

    @pl.when(s == pl.num_programs(1) - 1)
    def _():
        ns_ref[...] = carry_ref[...]


def _ffn_prompt(x3d, state, g, w_up, conv_w, conv_b, w_down, g_final=None, ts=512, chunk=256):
    batch, seq, d = x3d.shape
    d_ff = w_down.shape[0]
    assert d_ff % chunk == 0 and seq % ts == 0
    final_norm = g_final is not None
    args = [x3d, state, g.reshape(1, d), w_up, conv_w, conv_b.reshape(1, 2 * d_ff), w_down]
    specs = [pl.BlockSpec((None, ts, d), lambda b, s: (b, s, 0)),
             pl.BlockSpec((None, CONV_W - 1, 2 * d_ff), lambda b, s: (b, 0, 0)),
             _const_spec((1, d)), _const_spec((d, 2 * d_ff)), _const_spec((CONV_W, 2 * d_ff)),
             _const_spec((1, 2 * d_ff)), _const_spec((d_ff, d))]
    if final_norm:
        args.append(g_final.reshape(1, d))
        specs.append(_const_spec((1, d)))
    return pl.pallas_call(
        functools.partial(_ffn_prompt_kernel, ts=ts, d_ff=d_ff, chunk=chunk, final_norm=final_norm),
        grid=(batch, seq // ts),
        in_specs=specs,
        out_specs=[pl.BlockSpec((None, ts, d), lambda b, s: (b, s, 0)),
                   pl.BlockSpec((None, CONV_W - 1, 2 * d_ff), lambda b, s: (b, 0, 0))],
        out_shape=[jax.ShapeDtypeStruct((batch, seq, d), _F32),
                   jax.ShapeDtypeStruct((batch, CONV_W - 1, 2 * d_ff), _F32)],
        scratch_shapes=[pltpu.VMEM((CONV_W - 1, 2 * d_ff), _F32), pltpu.VMEM((ts, d_ff), _BF)],
        compiler_params=_params(2),
        name="conv_ffn_prompt",
    )(*args)


def _ffn_sample_kernel(*refs, d_ff, chunk, final_norm):
    if final_norm:
        x_ref, p2_ref, p1_ref, g_ref, wup_ref, cw_ref, cb_ref, wdn_ref, gf_ref, o_ref, u_ref, act_ref = refs
    else:
        x_ref, p2_ref, p1_ref, g_ref, wup_ref, cw_ref, cb_ref, wdn_ref, o_ref, u_ref, act_ref = refs
        gf_ref = None
    x = x_ref[...]
    hb = _rms(x, g_ref[...]).astype(_BF)

    def up(c0):
        cs = slice(c0, c0 + chunk)
        u = jnp.dot(hb, wup_ref[:, cs], preferred_element_type=_F32)
        u_ref[:, cs] = u
        return u, p1_ref[:, cs], p2_ref[:, cs]

    for c in range(d_ff // chunk):
        cg, cv = c * chunk, d_ff + c * chunk
        act_ref[:, cg:cg + chunk] = _ffn_gate(*up(cg), *up(cv), cw_ref, cb_ref, cg, cv, chunk)
    y = x + jnp.dot(act_ref[...], wdn_ref[...], preferred_element_type=_F32)
    if final_norm:
        y = _rms(y, gf_ref[...])
    o_ref[...] = y


def _ffn_sample(x2d, prev2, prev1, g, w_up, conv_w, conv_b, w_down, g_final=None, chunk=256):
    db, d = x2d.shape
    d_ff = w_down.shape[0]
    final_norm = g_final is not None
    args = [x2d, prev2, prev1, g.reshape(1, d), w_up, conv_w, conv_b.reshape(1, 2 * d_ff), w_down]
    specs = [_const_spec((db, d)), _const_spec((db, 2 * d_ff)), _const_spec((db, 2 * d_ff)),
             _const_spec((1, d)), _const_spec((d, 2 * d_ff)), _const_spec((CONV_W, 2 * d_ff)),
             _const_spec((1, 2 * d_ff)), _const_spec((d_ff, d))]
    if final_norm:
        args.append(g_final.reshape(1, d))
        specs.append(_const_spec((1, d)))
    return pl.pallas_call(
        functools.partial(_ffn_sample_kernel, d_ff=d_ff, chunk=chunk, final_norm=final_norm),
        grid=(1,),
        in_specs=specs,
        out_specs=[pl.BlockSpec((db, d), lambda i: (0, 0)), pl.BlockSpec((db, 2 * d_ff), lambda i: (0, 0))],
        out_shape=[jax.ShapeDtypeStruct((db, d), _F32), jax.ShapeDtypeStruct((db, 2 * d_ff), _F32)],
        scratch_shapes=[pltpu.VMEM((db, d_ff), _BF)],
        compiler_params=_params(1),
        name="conv_ffn_sample",
    )(*args)


def _mla_proj_kernel(x_ref, ck_ref, sk_ref, g_ref, win_ref, gq_ref, gkv_ref, wq_ref, wk_ref, wv_ref,
                     q_ref, k_ref, v_ref, ckv_ref, kr_ref):
    hb = _rms(x_ref[...], g_ref[...]).astype(_BF)
    proj = jnp.dot(hb, win_ref[...], preferred_element_type=_F32)
    c_q = _rms(proj[:, :Q_LORA], gq_ref[...])
    c_kv = _rms(proj[:, Q_LORA:Q_LORA + KV_LORA], gkv_ref[...])
    ckv_ref[...] = c_kv
    off = Q_LORA + KV_LORA
    ck, sk = ck_ref[...], sk_ref[...]
    kr_pad = proj[:, off:off + LANES] * ck + proj[:, off + LANES:off + 2 * LANES] * sk
    kr_ref[...] = kr_pad[:, QK_NOPE:QK_NOPE + QK_ROPE]
    scale = (QK_NOPE + QK_ROPE) ** -0.5
    nope = (lax.broadcasted_iota(jnp.int32, ck.shape, 1) < QK_NOPE).astype(_F32)
    cq_tab = scale * (ck + nope)
    sq_tab = scale * sk
    cqb = c_q.astype(_BF)
    ckvb = c_kv.astype(_BF)
    hw = H_B * LANES
    for h in range(H_B):
        sl = slice(h * LANES, (h + 1) * LANES)
        qa = jnp.dot(cqb, wq_ref[:, sl], preferred_element_type=_F32)
        qs = jnp.dot(cqb, wq_ref[:, hw + h * LANES:hw + (h + 1) * LANES], preferred_element_type=_F32)
        q_ref[h] = (qa * cq_tab + qs * sq_tab).astype(_BF)
        kn = jnp.dot(ckvb, wk_ref[:, sl], preferred_element_type=_F32)
        k_ref[h] = (kn + kr_pad).astype(_BF)
    for p in range(H_B // 2):
        sl = slice(p * LANES, (p + 1) * LANES)
        v_ref[p] = jnp.dot(ckvb, wv_ref[:, sl], preferred_element_type=_F32).astype(_BF)


def _mla_proj(x2d, ck_tab, sk_tab, g, w_in_ext, g_q, g_kv, w_q_both, w_k_pad, w_v, n_seq, tm):
    m, d = x2d.shape
    rows = m // n_seq
    tm = min(tm, rows)
    tps = rows // tm
    x_map = lambda s, b: (b * tps + s, 0)
    t_map = lambda s, b: (s, 0)
    return pl.pallas_call(
        _mla_proj_kernel,
        grid=(tps, n_seq),
        in_specs=[pl.BlockSpec((tm, d), x_map), pl.BlockSpec((tm, LANES), t_map), pl.BlockSpec((tm, LANES), t_map),
                  _const_spec((1, d)), _const_spec(w_in_ext.shape), _const_spec((1, Q_LORA)),
                  _const_spec((1, KV_LORA)), _const_spec(w_q_both.shape), _const_spec(w_k_pad.shape),
                  _const_spec(w_v.shape)],
        out_specs=[pl.BlockSpec((H_B, tm, LANES), lambda s, b: (0, b * tps + s, 0)),
                   pl.BlockSpec((H_B, tm, LANES), lambda s, b: (0, b * tps + s, 0)),
                   pl.BlockSpec((H_B // 2, tm, LANES), lambda s, b: (0, b * tps + s, 0)),
                   pl.BlockSpec((tm, KV_LORA), x_map), pl.BlockSpec((tm, QK_ROPE), x_map)],
        out_shape=[jax.ShapeDtypeStruct((H_B, m, LANES), _BF), jax.ShapeDtypeStruct((H_B, m, LANES), _BF),
                   jax.ShapeDtypeStruct((H_B // 2, m, LANES), _BF),
                   jax.ShapeDtypeStruct((m, KV_LORA), _F32), jax.ShapeDtypeStruct((m, QK_ROPE), _F32)],
        compiler_params=_params(2),
        name="mla_proj",
    )(x2d, ck_tab, sk_tab, g.reshape(1, d), w_in_ext, g_q.reshape(1, Q_LORA), g_kv.reshape(1, KV_LORA),
      w_q_both, w_k_pad, w_v)


def _mla_prompt_kernel(q_ref, k_ref, v_ref, o_ref, m_ref, l_ref, acc_ref, *, tq):
    qi = pl.program_id(2)
    m_ref[...] = jnp.full(m_ref.shape, -jnp.inf, _F32)
    l_ref[...] = jnp.zeros(l_ref.shape, _F32)
    acc_ref[...] = jnp.zeros(acc_ref.shape, _F32)
    lo = lax.broadcasted_iota(jnp.int32, (tq, LANES), 1) < V_HEAD
    row = lax.broadcasted_iota(jnp.int32, (tq, tq), 0)
    colk = lax.broadcasted_iota(jnp.int32, (tq, tq), 1)
    causal = colk <= row

    def step(kj, masked):
        ks = pl.multiple_of(kj * tq, tq)
        vblk = v_ref[pl.ds(ks, tq), :]
        alphas, pvs = [], []
        for hh in range(2):
            kblk = k_ref[hh, pl.ds(ks, tq), :]
            s = lax.dot_general(q_ref[hh], kblk, (((1,), (1,)), ((), ())), preferred_element_type=_F32)
            if masked:
                s = jnp.where(causal, s, NEG)
            m_prev = m_ref[hh]
            m_new = jnp.maximum(m_prev, jnp.max(s, axis=-1, keepdims=True))
            alpha = jnp.exp(m_prev - m_new)
            p = jnp.exp(s - m_new)
            l_ref[hh] = alpha * l_ref[hh] + jnp.sum(p, axis=-1, keepdims=True)
            m_ref[hh] = m_new
            alphas.append(alpha)
            pvs.append(jnp.dot(p.astype(_BF), vblk, preferred_element_type=_F32))
        acc_ref[...] = jnp.where(lo, alphas[0], alphas[1]) * acc_ref[...] + jnp.where(lo, pvs[0], pvs[1])

    def body(kj, carry):
        step(kj, False)
        return carry

    lax.fori_loop(0, qi, body, 0)
    step(qi, True)
    o_ref[...] = (acc_ref[...] / jnp.where(lo, l_ref[0], l_ref[1])).astype(o_ref.dtype)


def _mla_prompt_attn(q, k, v, batch, seq, tq=256):
    m = batch * seq
    nq = seq // tq
    return pl.pallas_call(
        functools.partial(_mla_prompt_kernel, tq=tq),
        grid=(batch, H_B // 2, nq),
        in_specs=[pl.BlockSpec((2, tq, LANES), lambda b, hp, i: (hp, b * nq + i, 0)),
                  pl.BlockSpec((2, seq, LANES), lambda b, hp, i: (hp, b, 0)),
                  pl.BlockSpec((None, seq, LANES), lambda b, hp, i: (hp, b, 0))],
        out_specs=pl.BlockSpec((tq, LANES), lambda b, hp, i: (b * nq + i, hp)),
        out_shape=jax.ShapeDtypeStruct((m, H_B * V_HEAD), _BF),
        scratch_shapes=[pltpu.VMEM((2, tq, 1), _F32), pltpu.VMEM((2, tq, 1), _F32), pltpu.VMEM((tq, LANES), _F32)],
        compiler_params=_params(3),
        name="mla_prompt_attn",
    )(q, k, v)


def _mla_absorb_kernel(q_ref, wuk_ref, o_ref):
    for h in range(H_B):
        o_ref[h] = jnp.dot(q_ref[h], wuk_ref[h], preferred_element_type=_F32).astype(o_ref.dtype)


def _mla_absorb(q, w_uk_t_pad):
    db = q.shape[1]
    return pl.pallas_call(
        _mla_absorb_kernel,
        grid=(1,),
        in_specs=[_const_spec(q.shape), _const_spec(w_uk_t_pad.shape)],
        out_specs=pl.BlockSpec((H_B, db, KV_LORA), lambda i: (0, 0, 0)),
        out_shape=jax.ShapeDtypeStruct((H_B, db, KV_LORA), _BF),
        compiler_params=_params(1),
        name="mla_absorb_q",
    )(q, w_uk_t_pad)


def _mla_sample_kernel(pt_ref, qlat_ref, qrope_ref, cnew_ref, krnew_ref, *refs, pages):
    ckv_refs = refs[:pages]
    kr_refs = refs[pages:2 * pages]
    o_ref, m_ref, l_ref, acc_ref = refs[2 * pages:]
    j = pl.program_id(1)

    @pl.when(j == 0)
    def _():
        m_ref[...] = jnp.full(m_ref.shape, -jnp.inf, _F32)
        l_ref[...] = jnp.zeros(l_ref.shape, _F32)
        acc_ref[...] = jnp.zeros(acc_ref.shape, _F32)

    qlat = qlat_ref[...]
    qrope = qrope_ref[...]

    def update(s, weighted_values):
        m_prev = m_ref[...]
        m_new = jnp.maximum(m_prev, jnp.max(s, axis=-1, keepdims=True))
        alpha = jnp.exp(m_prev - m_new)
        p = jnp.exp(s - m_new)
        l_ref[...] = alpha * l_ref[...] + jnp.sum(p, axis=-1, keepdims=True)
        acc_ref[...] = alpha * acc_ref[...] + weighted_values(p.astype(_BF))
        m_ref[...] = m_new

    for t in range(pages):
        ckv = ckv_refs[t][...].astype(_BF)
        kr = kr_refs[t][...].astype(_BF)
        s = lax.dot_general(qlat, ckv, (((1,), (1,)), ((), ())), preferred_element_type=_F32) \
            + lax.dot_general(qrope, kr, (((1,), (1,)), ((), ())), preferred_element_type=_F32)
        update(s, lambda pb, ckv=ckv: jnp.dot(pb, ckv, preferred_element_type=_F32))

    @pl.when(j == pl.num_programs(1) - 1)
    def _():
        cn = cnew_ref[...].astype(_BF).astype(_F32)
        krn = krnew_ref[...].astype(_BF).astype(_F32)
        s = jnp.sum(qlat.astype(_F32) * cn, axis=-1, keepdims=True) \
            + jnp.sum(qrope.astype(_F32) * krn, axis=-1, keepdims=True)
        update(s, lambda pb: pb.astype(_F32) * cn)
        o_ref[...] = (acc_ref[...] / l_ref[...]).astype(o_ref.dtype)


def _mla_sample_attn(page_table, qlat, qrope, c_new, kr_new, ckv_pool, kr_pool, pages=16):
    db, n_pages = page_table.shape
    page = ckv_pool.shape[1]
    assert n_pages % pages == 0
    pt_flat = page_table.reshape(-1)

    def page_map(t):
        return lambda b, j, pt: (pt[b * n_pages + j * pages + t], 0, 0)

    per_b = lambda b, j, pt: (b, 0, 0)
    in_specs = [pl.BlockSpec((None, H_B, KV_LORA), per_b), pl.BlockSpec((None, H_B, QK_ROPE), per_b),
                pl.BlockSpec((None, 1, KV_LORA), per_b), pl.BlockSpec((None, 1, QK_ROPE), per_b)]
    in_specs += [pl.BlockSpec((None, page, KV_LORA), page_map(t)) for t in range(pages)]
    in_specs += [pl.BlockSpec((None, page, QK_ROPE), page_map(t)) for t in range(pages)]
    grid_spec = pltpu.PrefetchScalarGridSpec(
        num_scalar_prefetch=1,
        grid=(db, n_pages // pages),
        in_specs=in_specs,
        out_specs=pl.BlockSpec((None, H_B, KV_LORA), per_b),
        scratch_shapes=[pltpu.VMEM((H_B, 1), _F32), pltpu.VMEM((H_B, 1), _F32), pltpu.VMEM((H_B, KV_LORA), _F32)],
    )
    return pl.pallas_call(
        functools.partial(_mla_sample_kernel, pages=pages),
        grid_spec=grid_spec,
        out_shape=jax.ShapeDtypeStruct((db, H_B, KV_LORA), _BF),
        compiler_params=_params(2),
        name="mla_sample_attn",
    )(pt_flat, qlat, qrope, c_new, kr_new, *([ckv_pool] * pages), *([kr_pool] * pages))


def _mla_unabsorb_kernel(o_ref, wuv_ref, out_ref):
    for p in range(H_B // 2):
        y = jnp.dot(o_ref[2 * p], wuv_ref[2 * p], preferred_element_type=_F32) \
            + jnp.dot(o_ref[2 * p + 1], wuv_ref[2 * p + 1], preferred_element_type=_F32)
        out_ref[:, p * LANES:(p + 1) * LANES] = y.astype(out_ref.dtype)


def _mla_unabsorb(o_lat_hm, w_uv_pair):
    db = o_lat_hm.shape[1]
    return pl.pallas_call(
        _mla_unabsorb_kernel,
        grid=(1,),
        in_specs=[_const_spec(o_lat_hm.shape), _const_spec(w_uv_pair.shape)],
        out_specs=pl.BlockSpec((db, H_B * V_HEAD), lambda i: (0, 0)),
        out_shape=jax.ShapeDtypeStruct((db, H_B * V_HEAD), _BF),
        compiler_params=_params(1),
        name="mla_unabsorb_o",
    )(o_lat_hm, w_uv_pair)


def _t5_bucket(dist):
    n = np.asarray(dist)
    max_exact = N_BUCKETS // 2
    large = max_exact + (np.log(np.maximum(n, 1) / max_exact) / np.log(T5_MAX_DISTANCE / max_exact)
                         * (N_BUCKETS - max_exact)).astype(np.int32)
    large = np.minimum(large, N_BUCKETS - 1)
    return np.where(n < max_exact, n, large).astype(np.int32)


def _dsw_biases(table):
    out = []
    for g, (w, d) in enumerate(zip(WINDOWS, DILATIONS)):
        buckets = _t5_bucket(np.arange(w // d + 1) * d)
        out.append(table[buckets][:, g * H_A:(g + 1) * H_A].T.astype(_F32))
    return out


def _dsw_bias_blocks(bias_g):
    rel = np.arange(BLK)[:, None] + BLK - np.arange(2 * BLK)[None, :]
    return bias_g[:, np.clip(rel, 0, BLK)]


def _rope_pad_tables(pos):
    inv = jnp.asarray(ROPE_BASE ** (-np.arange(0, QK_ROPE, 2) / QK_ROPE), dtype=_F32)
    ang = pos.astype(_F32)[:, None] * inv[None, :]
    cos, sin = jnp.cos(ang), jnp.sin(ang)
    n = pos.shape[0]
    zl = jnp.zeros((n, QK_NOPE), _F32)
    zr = jnp.zeros((n, LANES - QK_NOPE - QK_ROPE), _F32)
    return (jnp.concatenate([zl, cos, cos, zr], axis=1), jnp.concatenate([zl, -sin, sin, zr], axis=1))


def _mla_weights(w_in, w_q, w_kv):
    d = w_in.shape[0]
    half = QK_ROPE // 2
    off = Q_LORA + KV_LORA
    kr = w_in[:, off:]
    kr_sw = jnp.concatenate([kr[:, half:], kr[:, :half]], axis=1)
    zl = jnp.zeros((d, QK_NOPE), w_in.dtype)
    zr = jnp.zeros((d, LANES - QK_NOPE - QK_ROPE), w_in.dtype)
    w_in_ext = jnp.concatenate([w_in[:, :off], zl, kr, zr, zl, kr_sw, zr], axis=1).astype(_BF)

    dh = QK_NOPE + QK_ROPE
    wq3 = w_q.reshape(Q_LORA, H_B, dh)
    pad = jnp.zeros((Q_LORA, H_B, LANES - dh), w_q.dtype)
    wq_pad = jnp.concatenate([wq3, pad], axis=2)
    wq_sw = jnp.concatenate([jnp.zeros((Q_LORA, H_B, QK_NOPE), w_q.dtype), wq3[:, :, QK_NOPE + half:],
                             wq3[:, :, QK_NOPE:QK_NOPE + half], pad], axis=2)
    w_q_both = jnp.concatenate([wq_pad.reshape(Q_LORA, H_B * LANES), wq_sw.reshape(Q_LORA, H_B * LANES)],
                               axis=1).astype(_BF)

    wkv3 = w_kv.reshape(KV_LORA, H_B, QK_NOPE + V_HEAD)
    w_uk, w_uv = wkv3[:, :, :QK_NOPE], wkv3[:, :, QK_NOPE:]
    w_k_pad = jnp.concatenate([w_uk, jnp.zeros((KV_LORA, H_B, LANES - QK_NOPE), w_kv.dtype)], axis=2)
    w_k_pad = w_k_pad.reshape(KV_LORA, H_B * LANES).astype(_BF)
    w_v = w_uv.reshape(KV_LORA, H_B * V_HEAD).astype(_BF)
    w_uk_t = jnp.transpose(w_uk, (1, 2, 0))
    w_uk_t_pad = jnp.concatenate([w_uk_t, jnp.zeros((H_B, LANES - QK_NOPE, KV_LORA), w_kv.dtype)], axis=1)
    w_uv_h = jnp.transpose(w_uv, (1, 0, 2))
    zero = jnp.zeros_like(w_uv_h)
    even = jnp.concatenate([w_uv_h, zero], axis=2)
    odd = jnp.concatenate([zero, w_uv_h], axis=2)
    is_even = (jnp.arange(H_B) % 2 == 0)[:, None, None]
    w_uv_pair = jnp.where(is_even, even, odd)
    return w_in_ext, w_q_both, w_k_pad, w_v, w_uk_t_pad.astype(_BF), w_uv_pair.astype(_BF)


def kernel(x_prompt, x_sample, cache_dsw_g0, cache_dsw_g1, cache_dsw_g2, cache_mla_ckv, cache_mla_kr,
           state_ffn_conv, page_table, rel_bias_table, norm_mix, norm_ffn, norm_final, w_qkv_dsw, w_o_dsw,
           w_in_mla, g_q_mla, g_kv_mla, w_q_mla, w_kv_mla, w_o_mla, w_up_ffn, conv_w_ffn, conv_b_ffn, w_down_ffn):
    batch, seq, d = x_prompt.shape
    db, t_new, _ = x_sample.shape
    assert t_new == 1
    depth = norm_mix.shape[0]
    d_ff = w_down_ffn.shape[1]
    past_len = page_table.shape[1] * cache_mla_ckv.shape[2]
    m = batch * seq
    dsw_caches = (cache_dsw_g0, cache_dsw_g1, cache_dsw_g2)
    biases = _dsw_biases(rel_bias_table)
    bias_blocks = [_dsw_bias_blocks(bg) for bg in biases]

    xp = x_prompt.reshape(m, d)
    xs = x_sample.reshape(db, d)
    dsw_new_p = [[] for _ in range(N_GROUPS)]
    dsw_new_s = [[] for _ in range(N_GROUPS)]
    ckv_p, ckv_s, kr_p, kr_s, conv_p, conv_s = [], [], [], [], [], []
    zero_state = jnp.zeros((batch, CONV_W - 1, 2 * d_ff), _F32)

    for i in range(depth):
        if i % 2 == 0:
            a = i // 2
            w_qkv = w_qkv_dsw[a].astype(_BF)
            w_o = w_o_dsw[a].astype(_BF)
            qkv, c0, c1, c2 = _qkv_prompt(xp, norm_mix[i], w_qkv, batch, seq)
            for g, c in enumerate((c0, c1, c2)):
                dsw_new_p[g].append(c.reshape(batch, min(WINDOWS[g], seq), 2, H_A, DH_A))
            outs, lses = zip(*[_dsw_prompt_attn(qkv, bias_blocks[g], g, batch, seq) for g in range(N_GROUPS)])
            xp = _dsw_merge(outs, lses, w_o, xp)
            qkv_s = _linear(xs, w_qkv, norm_g=norm_mix[i], name="dsw_qkv_sample")
            outs, lses = [], []
            for g in range(N_GROUPS):
                o, lse, new = _dsw_sample(qkv_s, dsw_caches[g][a], biases[g], g)
                outs.append(o)
                lses.append(lse)
                dsw_new_s[g].append(new.reshape(db, WINDOWS[g], 2, H_A, DH_A))
            xs = _dsw_merge(outs, lses, w_o, xs)
        else:
            b = i // 2
            w_in_ext, w_q_both, w_k_pad, w_v, w_uk_t_pad, w_uv_pair = _mla_weights(
                w_in_mla[b], w_q_mla[b], w_kv_mla[b])
            w_o = w_o_mla[b].astype(_BF)
            ck_tab, sk_tab = _rope_pad_tables(jnp.arange(seq))
            q, k, v, ckv, kr = _mla_proj(xp, ck_tab, sk_tab, norm_mix[i], w_in_ext, g_q_mla[b], g_kv_mla[b],
                                         w_q_both, w_k_pad, w_v, n_seq=batch, tm=512)
            ckv_p.append(ckv.reshape(batch, seq, KV_LORA))
            kr_p.append(kr.reshape(batch, seq, QK_ROPE))
            o = _mla_prompt_attn(q, k, v, batch, seq)
            xp = _linear(o, w_o, residual=xp, name="mla_out_prompt")
            ck_s, sk_s = _rope_pad_tables(jnp.full((db,), past_len, jnp.int32))
            q, _, _, ckv, kr = _mla_proj(xs, ck_s, sk_s, norm_mix[i], w_in_ext, g_q_mla[b], g_kv_mla[b],
                                         w_q_both, w_k_pad, w_v, n_seq=1, tm=db)
            ckv_s.append(ckv.reshape(db, 1, KV_LORA))
            kr_s.append(kr.reshape(db, 1, QK_ROPE))
            qlat = jnp.transpose(_mla_absorb(q, w_uk_t_pad), (1, 0, 2))
            qrope = jnp.transpose(q[:, :, QK_NOPE:QK_NOPE + QK_ROPE], (1, 0, 2))
            o_lat = _mla_sample_attn(page_table, qlat, qrope, ckv.reshape(db, 1, KV_LORA),
                                     kr.reshape(db, 1, QK_ROPE), cache_mla_ckv[b], cache_mla_kr[b])
            o = _mla_unabsorb(jnp.transpose(o_lat, (1, 0, 2)), w_uv_pair)
            xs = _linear(o, w_o, residual=xs, name="mla_out_sample")

        g_final = norm_final if i == depth - 1 else None
        w_up = w_up_ffn[i].astype(_BF)
        w_dn = w_down_ffn[i].astype(_BF)
        yp, cp = _ffn_prompt(xp.reshape(batch, seq, d), zero_state, norm_ffn[i], w_up, conv_w_ffn[i],
                             conv_b_ffn[i], w_dn, g_final)
        xp = yp.reshape(m, d)
        conv_p.append(cp)
        prev2, prev1 = state_ffn_conv[i, :, 0], state_ffn_conv[i, :, 1]
        xs, u_s = _ffn_sample(xs, prev2, prev1, norm_ffn[i], w_up, conv_w_ffn[i], conv_b_ffn[i], w_dn, g_final)
        conv_s.append(jnp.stack([prev1, u_s], axis=1))

    def stack(per_layer):
        return per_layer[0][None] if len(per_layer) == 1 else jnp.stack(per_layer)

    y_prompt = xp.reshape(batch, seq, d)
    y_sample = xs.reshape(db, 1, d)
    return (y_prompt, y_sample,
            stack(dsw_new_p[0]), stack(dsw_new_s[0]), stack(dsw_new_p[1]), stack(dsw_new_s[1]),
            stack(dsw_new_p[2]), stack(dsw_new_s[2]),
            stack(ckv_p), stack(ckv_s), stack(kr_p), stack(kr_s), stack(conv_p), stack(conv_s))
```

```python
import functools

import numpy as np
import jax
import jax.numpy as jnp
from jax import lax
from jax.experimental import pallas as pl
from jax.experimental.pallas import tpu as pltpu

WINDOWS = (128, 512, 2048)
DILATIONS = (1, 4, 16)
N_GROUPS = 3
H_A = 8
DH_A = 64
D_ATT = H_A * DH_A
BLK = 128
N_BUCKETS = 32
T5_MAX_DISTANCE = 2048
H_B = 16
Q_LORA = 384
KV_LORA = 256
QK_NOPE = 64
QK_ROPE = 32
V_HEAD = 64
ROPE_BASE = 10000.0
CONV_W = 3
EPS = 1e-6
NEG = -1e30
LOG2E = 1.4426950408889634

LANES = 128
VMEM_LIMIT = 56 * 1024 * 1024

_BF = jnp.bfloat16
_F32 = jnp.float32


def _params(n_axes, vmem=VMEM_LIMIT):
    return pltpu.CompilerParams(dimension_semantics=("arbitrary",) * n_axes, vmem_limit_bytes=vmem)


def _const_spec(shape):
    nd = len(shape)
    return pl.BlockSpec(shape, lambda *_: (0,) * nd, pipeline_mode=pl.Buffered(1))


def _rms(x, g):
    return x * lax.rsqrt(jnp.mean(x * x, axis=-1, keepdims=True) + EPS) * g


def _linear_kernel(*refs, has_norm, has_res, n_chunk):
    it = iter(refs)
    x_ref = next(it)
    g_ref = next(it) if has_norm else None
    w_ref = next(it)
    r_ref = next(it) if has_res else None
    o_ref = next(it)
    x = x_ref[...]
    if has_norm:
        x = _rms(x.astype(_F32), g_ref[...])
    xb = x.astype(_BF)
    n = w_ref.shape[1]
    for c0 in range(0, n, n_chunk):
        y = jnp.dot(xb, w_ref[:, c0:c0 + n_chunk], preferred_element_type=_F32)
        if has_res:
            y = y + r_ref[:, c0:c0 + n_chunk]
        o_ref[:, c0:c0 + n_chunk] = y.astype(o_ref.dtype)


def _linear(x, w, *, norm_g=None, residual=None, out_dtype=_F32, tm=512, name="linear"):
    m, k = x.shape
    n = w.shape[1]
    tm = min(tm, m)
    assert m % tm == 0
    n_chunk = 512 if n % 512 == 0 else n
    args, specs = [x], [pl.BlockSpec((tm, k), lambda i: (i, 0))]
    if norm_g is not None:
        args.append(norm_g.reshape(1, k))
        specs.append(_const_spec((1, k)))
    args.append(w)
    specs.append(_const_spec((k, n)))
    if residual is not None:
        args.append(residual)
        specs.append(pl.BlockSpec((tm, n), lambda i: (i, 0)))
    return pl.pallas_call(
        functools.partial(_linear_kernel, has_norm=norm_g is not None, has_res=residual is not None,
                          n_chunk=n_chunk),
        grid=(m // tm,),
        in_specs=specs,
        out_specs=pl.BlockSpec((tm, n), lambda i: (i, 0)),
        out_shape=jax.ShapeDtypeStruct((m, n), out_dtype),
        compiler_params=_params(1),
        name=name,
    )(*args)


_PAIRS = H_A // 2


def _qkv_prompt_kernel(x_ref, g_ref, w_ref, q0_ref, q1_ref, q2_ref, c0_ref, c1_ref, c2_ref, scr_ref, *, tm,
                       tiles_per_seq, keeps):
    last_tile = pl.program_id(1) == tiles_per_seq - 1
    hb = _rms(x_ref[...], g_ref[...]).astype(_BF)
    out_refs = (q0_ref, q1_ref, q2_ref)
    cache_refs = (c0_ref, c1_ref, c2_ref)
    for g in range(N_GROUPS):
        dil = DILATIONS[g]
        rows = tm // dil
        oref, cref, keep = out_refs[g], cache_refs[g], keeps[g]
        for part in range(3):
            col = (g * 3 + part) * D_ATT
            y = jnp.dot(hb, w_ref[:, col:col + D_ATT], preferred_element_type=_F32)
            if part == 0:
                y = y * (DH_A ** -0.5)
            if dil == 1:
                oref[0, :, part * D_ATT:(part + 1) * D_ATT] = y.astype(_BF)
            else:
                for p in range(_PAIRS):
                    scr_ref[p] = y[:, p * LANES:(p + 1) * LANES]
                for r in range(dil):
                    for p in range(_PAIRS):
                        c0 = part * D_ATT + p * LANES
                        oref[r, :, c0:c0 + LANES] = scr_ref[p, pl.ds(r, rows, stride=dil), :].astype(_BF)
            if part > 0:
                dst = slice((part - 1) * D_ATT, part * D_ATT)
                if keep == tm * tiles_per_seq:
                    cref[dst, :] = y.T
                else:
                    @pl.when(last_tile)
                    def _(cref=cref, y=y, dst=dst, keep=keep):
                        cref[dst, :] = y[tm - keep:, :].T


def _qkv_prompt(x3d, g, w, tm=512):
    batch, seq, d = x3d.shape
    n = w.shape[1]
    tps = seq // tm
    keeps = tuple(min(wd, seq) for wd in WINDOWS)
    out_shapes, out_specs = [], []
    for dil in DILATIONS:
        assert tm % (16 * dil) == 0
        out_shapes.append(jax.ShapeDtypeStruct((batch, dil, seq // dil, 3 * D_ATT), _BF))
        out_specs.append(pl.BlockSpec((None, dil, tm // dil, 3 * D_ATT), lambda b, s: (b, 0, s, 0)))
    for keep in keeps:
        assert keep == seq or keep <= tm
        out_shapes.append(jax.ShapeDtypeStruct((batch, 2 * D_ATT, keep), _F32))
        if keep == seq:
            out_specs.append(pl.BlockSpec((None, 2 * D_ATT, tm), lambda b, s: (b, 0, s)))
        else:
            out_specs.append(pl.BlockSpec((None, 2 * D_ATT, keep), lambda b, s: (b, 0, 0)))
    return pl.pallas_call(
        functools.partial(_qkv_prompt_kernel, tm=tm, tiles_per_seq=tps, keeps=keeps),
        grid=(batch, tps),
        in_specs=[pl.BlockSpec((None, tm, d), lambda b, s: (b, s, 0)), _const_spec((1, d)), _const_spec((d, n))],
        out_specs=out_specs,
        out_shape=out_shapes,
        scratch_shapes=[pltpu.VMEM((_PAIRS, tm, LANES), _F32)],
        compiler_params=_params(2),
        name="dsw_qkv_prompt",
    )(x3d, g.reshape(1, d), w)


def _dsw_prompt_kernel(qkv_ref, bias_ref, o_ref, lse_ref, *, dil, nb):
    nk = 2 * BLK if nb > 1 else BLK
    qi = lax.broadcasted_iota(jnp.int32, (BLK, nk), 0)
    ki = lax.broadcasted_iota(jnp.int32, (BLK, nk), 1)
    rel = qi + (nk - BLK) - ki
    band = (rel >= 0) & (rel <= BLK)
    lo = lax.broadcasted_iota(jnp.int32, (BLK, LANES), 1) < DH_A

    def block(idx, carry):
        r = idx // nb
        n = idx % nb
        q0 = pl.multiple_of(n * BLK, BLK)
        if nb > 1:
            p0 = pl.multiple_of(jnp.maximum(n - 1, 0) * BLK, BLK)
            valid = band & jnp.logical_or(ki >= BLK, n > 0)
        else:
            valid = band
        start = q0 * dil + r
        for pair in range(_PAIRS):
            cq, ck, cv = pair * LANES, D_ATT + pair * LANES, 2 * D_ATT + pair * LANES
            qp = qkv_ref[r, pl.ds(q0, BLK), cq:cq + LANES]
            kp = qkv_ref[r, pl.ds(q0, BLK), ck:ck + LANES]
            vp = qkv_ref[r, pl.ds(q0, BLK), cv:cv + LANES]
            if nb > 1:
                kp = jnp.concatenate([qkv_ref[r, pl.ds(p0, BLK), ck:ck + LANES], kp], axis=0)
                vp = jnp.concatenate([qkv_ref[r, pl.ds(p0, BLK), cv:cv + LANES], vp], axis=0)
            outs, lses = [], []
            for hh in range(2):
                sel = lo if hh == 0 else jnp.logical_not(lo)
                qm = jnp.where(sel, qp, jnp.zeros_like(qp))
                s = lax.dot_general(qm, kp, _NT, preferred_element_type=_F32)
                s = jnp.where(valid, s + bias_ref[pair * 2 + hh], NEG)
                m = jnp.max(s, axis=-1, keepdims=True)
                p = jnp.exp(s - m)
                l = jnp.sum(p, axis=-1, keepdims=True)
                pv = jnp.dot(p.astype(_BF), vp, preferred_element_type=_F32)
                outs.append(pv / l)
                lses.append(m + jnp.log(l))
            o_pair = jnp.where(lo, outs[0], outs[1])
            lse_pair = jnp.where(lo, jnp.broadcast_to(lses[0], (BLK, LANES)),
                                 jnp.broadcast_to(lses[1], (BLK, LANES)))
            rows = pl.ds(start, BLK) if dil == 1 else pl.ds(start, BLK, stride=dil)
            o_ref[pair, rows, :] = o_pair
            lse_ref[pair, rows, :] = lse_pair
        return carry

    lax.fori_loop(0, dil * nb, block, 0)


def _dsw_prompt_attn(qkv_g, bias_full, g):
    batch, dil, sub, width = qkv_g.shape
    seq = dil * sub
    assert seq % WINDOWS[g] == 0 and dil == DILATIONS[g]
    nb = sub // BLK
    if nb == 1:
        bias_full = bias_full[:, :, BLK:]
    out = jax.ShapeDtypeStruct((batch, _PAIRS, seq, LANES), _F32)
    out_spec = pl.BlockSpec((None, _PAIRS, seq, LANES), lambda b: (b, 0, 0, 0))
    return pl.pallas_call(
        functools.partial(_dsw_prompt_kernel, dil=dil, nb=nb),
        grid=(batch,),
        in_specs=[pl.BlockSpec((None, dil, sub, width), lambda b: (b, 0, 0, 0)), _const_spec(bias_full.shape)],
        out_specs=[out_spec, out_spec],
        out_shape=[out, out],
        compiler_params=_params(1),
        name=f"dsw_prompt_attn_g{g}",
    )(qkv_g, bias_full)


def _merge_groups(os, ls):
    la, lb, lc = ls
    mx = jnp.maximum(jnp.maximum(la, lb), lc)
    ea, eb, ec = jnp.exp(la - mx), jnp.exp(lb - mx), jnp.exp(lc - mx)
    return (ea * os[0] + eb * os[1] + ec * os[2]) / (ea + eb + ec)


def _dsw_merge_slab_kernel(o0_ref, o1_ref, o2_ref, l0_ref, l1_ref, l2_ref, w_ref, x_ref, out_ref):
    o_refs, l_refs = (o0_ref, o1_ref, o2_ref), (l0_ref, l1_ref, l2_ref)
    slabs = [_merge_groups([r[p] for r in o_refs], [r[p] for r in l_refs]).astype(_BF) for p in range(_PAIRS)]
    o = jnp.concatenate(slabs, axis=1)
    out_ref[...] = x_ref[...] + jnp.dot(o, w_ref[...], preferred_element_type=_F32)


def _dsw_merge_slab(outs, lses, w_o, x3d, tm=512):
    batch, seq, d = x3d.shape
    spec = pl.BlockSpec((None, _PAIRS, tm, LANES), lambda b, s: (b, 0, s, 0))
    xspec = pl.BlockSpec((None, tm, d), lambda b, s: (b, s, 0))
    return pl.pallas_call(
        _dsw_merge_slab_kernel,
        grid=(batch, seq // tm),
        in_specs=[spec] * 6 + [_const_spec((D_ATT, d)), xspec],
        out_specs=xspec,
        out_shape=jax.ShapeDtypeStruct((batch, seq, d), _F32),
        compiler_params=_params(2),
        name="dsw_merge_out_prompt",
    )(*outs, *lses, w_o, x3d)


def _dsw_merge_kernel(o0_ref, o1_ref, o2_ref, l0_ref, l1_ref, l2_ref, w_ref, x_ref, out_ref):
    o = _merge_groups([o0_ref[...], o1_ref[...], o2_ref[...]], [l0_ref[...], l1_ref[...], l2_ref[...]])
    out_ref[...] = x_ref[...] + jnp.dot(o.astype(_BF), w_ref[...], preferred_element_type=_F32)


def _dsw_merge(outs, lses, w_o, x2d, tm=512):
    m, d = x2d.shape
    tm = min(tm, m)
    spec = pl.BlockSpec((tm, D_ATT), lambda i: (i, 0))
    return pl.pallas_call(
        _dsw_merge_kernel,
        grid=(m // tm,),
        in_specs=[spec] * 6 + [_const_spec((D_ATT, d)), pl.BlockSpec((tm, d), lambda i: (i, 0))],
        out_specs=pl.BlockSpec((tm, d), lambda i: (i, 0)),
        out_shape=jax.ShapeDtypeStruct((m, d), _F32),
        compiler_params=_params(1),
        name="dsw_merge_out",
    )(*outs, *lses, w_o, x2d)


_NT = (((1,), (1,)), ((), ()))


def _linear_t_kernel(x_ref, g_ref, wt_ref, o_ref, *, n_chunk):
    hb = _rms(x_ref[...], g_ref[...]).astype(_BF)
    for c0 in range(0, wt_ref.shape[0], n_chunk):
        o_ref[c0:c0 + n_chunk, :] = lax.dot_general(wt_ref[c0:c0 + n_chunk, :], hb, _NT,
                                                    preferred_element_type=_F32)


def _linear_t(x, norm_g, wt, name):
    m, k = x.shape
    n = wt.shape[0]
    return pl.pallas_call(
        functools.partial(_linear_t_kernel, n_chunk=512),
        grid=(1,),
        in_specs=[_const_spec((m, k)), _const_spec((1, k)), _const_spec((n, k))],
        out_specs=pl.BlockSpec((n, m), lambda i: (0, 0)),
        out_shape=jax.ShapeDtypeStruct((n, m), _F32),
        compiler_params=_params(1),
        name=name,
    )(x, norm_g.reshape(1, k), wt)


def _dsw_sample_kernel(qkv_ref, kvt_ref, cache_ref, bias_ref, b0_ref, o_ref, lse_ref, new_ref, *, g, window,
                       dil, rows):
    b = pl.program_id(0)
    base = g * 3 * D_ATT
    q = qkv_ref[:, base:base + D_ATT] * (DH_A ** -0.5)
    knew = qkv_ref[:, base + D_ATT:base + 2 * D_ATT]
    vnew = qkv_ref[:, base + 2 * D_ATT:base + 3 * D_ATT]
    head = lax.broadcasted_iota(jnp.int32, (H_A, D_ATT), 0)
    lane_head = lax.broadcasted_iota(jnp.int32, (H_A, D_ATT), 1) // DH_A
    hm = head == lane_head
    qrows = jnp.where(hm, jnp.broadcast_to(q, (H_A, D_ATT)), 0.0).astype(_BF)
    s = jnp.dot(qrows, cache_ref[0:D_ATT, :].astype(_BF), preferred_element_type=_F32)
    pos = lax.broadcasted_iota(jnp.int32, (H_A, window), 1)
    s = jnp.where((pos & (dil - 1)) == 0, s + bias_ref[...], NEG)
    prod = q.astype(_BF).astype(_F32) * knew.astype(_BF).astype(_F32)
    s_new = jnp.sum(jnp.where(hm, jnp.broadcast_to(prod, (H_A, D_ATT)), 0.0), axis=-1, keepdims=True) \
        + b0_ref[...]
    m = jnp.maximum(jnp.max(s, axis=-1, keepdims=True), s_new)
    pr = jnp.exp(s - m)
    p_new = jnp.exp(s_new - m)
    l = jnp.sum(pr, axis=-1, keepdims=True) + p_new
    o8 = lax.dot_general(pr.astype(_BF), cache_ref[D_ATT:2 * D_ATT, :].astype(_BF), _NT,
                         preferred_element_type=_F32)
    o8 = (o8 + p_new.astype(_BF).astype(_F32) * vnew.astype(_BF).astype(_F32)) / l
    lse8 = jnp.broadcast_to(m + jnp.log(l), (H_A, D_ATT))
    o_ref[...] = jnp.sum(jnp.where(hm, o8, 0.0), axis=0, keepdims=True)
    lse_ref[...] = jnp.sum(jnp.where(hm, lse8, 0.0), axis=0, keepdims=True)
    mine = lax.broadcasted_iota(jnp.int32, (rows, kvt_ref.shape[1]), 1) == b
    last = lax.broadcasted_iota(jnp.int32, (rows, LANES), 1) == LANES - 1
    for r0 in range(0, 2 * D_ATT, rows):
        rolled = pltpu.roll(cache_ref[r0:r0 + rows, :], window - 1, 1)
        col = jnp.sum(jnp.where(mine, kvt_ref[r0:r0 + rows, :], 0.0), axis=1, keepdims=True)
        if window > LANES:
            new_ref[r0:r0 + rows, :window - LANES] = rolled[:, :window - LANES]
        new_ref[r0:r0 + rows, window - LANES:] = jnp.where(last, col, rolled[:, window - LANES:])


def _dsw_sample(qkv_s, kvt_s, cache_t, bias_g, g):
    db = qkv_s.shape[0]
    window, dil = WINDOWS[g], DILATIONS[g]
    assert cache_t.shape[2] == window and dil & (dil - 1) == 0
    width = qkv_s.shape[-1]
    w = np.arange(window)
    back = np.where(w % dil == 0, (window - w) // dil, 0)
    bias_pos = bias_g[:, back]
    b0 = bias_g[:, 0:1]
    blk = (None, 2 * D_ATT, window)
    o, lse, new = pl.pallas_call(
        functools.partial(_dsw_sample_kernel, g=g, window=window, dil=dil, rows=128),
        grid=(db,),
        in_specs=[pl.BlockSpec((None, 1, width), lambda b: (b, 0, 0)),
                  pl.BlockSpec((None, 2 * D_ATT, db), lambda b: (g, 0, 0)),
                  pl.BlockSpec(blk, lambda b: (b, 0, 0)),
                  _const_spec((H_A, window)), _const_spec((H_A, 1))],
        out_specs=[pl.BlockSpec((None, 1, D_ATT), lambda b: (b, 0, 0)),
                   pl.BlockSpec((None, 1, D_ATT), lambda b: (b, 0, 0)),
                   pl.BlockSpec(blk, lambda b: (b, 0, 0))],
        out_shape=[jax.ShapeDtypeStruct((db, 1, D_ATT), _F32), jax.ShapeDtypeStruct((db, 1, D_ATT), _F32),
                   jax.ShapeDtypeStruct((db, 2 * D_ATT, window), _F32)],
        compiler_params=_params(1),
        name=f"dsw_sample_g{g}",
    )(qkv_s.reshape(db, 1, width), kvt_s, cache_t, bias_pos, b0)
    return o.reshape(db, D_ATT), lse.reshape(db, D_ATT), new


def _ffn_gate(ug, ug1, ug2, uv, uv1, uv2, cw_ref, cb_ref, cg, cv, chunk):
    def conv(u, u1, u2, c0):
        cs = slice(c0, c0 + chunk)
        return cb_ref[:, cs] + cw_ref[0:1, cs] * u2 + cw_ref[1:2, cs] * u1 + cw_ref[2:3, cs] * u
    yg = conv(ug, ug1, ug2, cg)
    yv = conv(uv, uv1, uv2, cv)
    return (yg * (1.0 / (1.0 + jnp.exp(-yg))) * yv).astype(_BF)


def _ffn_prompt_kernel(*refs, ts, d_ff, chunk, final_norm):
    if final_norm:
        x_ref, st_ref, g_ref, wup_ref, cw_ref, cb_ref, wdn_ref, gf_ref, o_ref, ns_ref, carry_ref, act_ref = refs
    else:
        x_ref, st_ref, g_ref, wup_ref, cw_ref, cb_ref, wdn_ref, o_ref, ns_ref, carry_ref, act_ref = refs
        gf_ref = None
    s = pl.program_id(1)

    @pl.when(s == 0)
    def _():
        carry_ref[...] = st_ref[...]

    x = x_ref[...]
    hb = _rms(x, g_ref[...]).astype(_BF)
    row = lax.broadcasted_iota(jnp.int32, (ts, chunk), 0)
    first, second = row == 0, row == 1

    def up(c0):
        u = jnp.dot(hb, wup_ref[:, c0:c0 + chunk], preferred_element_type=_F32)
        p2 = carry_ref[0:1, c0:c0 + chunk]
        p1 = carry_ref[1:2, c0:c0 + chunk]
        u1 = jnp.where(first, p1, pltpu.roll(u, 1, 0))
        u2 = jnp.where(first, p2, jnp.where(second, p1, pltpu.roll(u, 2, 0)))
        carry_ref[:, c0:c0 + chunk] = u[ts - 2:ts, :]
        return u, u1, u2

    for c in range(d_ff // chunk):
        cg, cv = c * chunk, d_ff + c * chunk
        act_ref[:, cg:cg + chunk] = _ffn_gate(*up(cg), *up(cv), cw_ref, cb_ref, cg, cv, chunk)
    y = x + jnp.dot(act_ref[...], wdn_ref[...], preferred_element_type=_F32)
    if final_norm:
        y = _rms(y, gf_ref[...])
    o_ref[...] = y

    @pl.when(s == pl.num_programs(1) - 1)
    def _():
        ns_ref[...] = carry_ref[...]


def _ffn_prompt(x3d, state, g, w_up, conv_w, conv_b, w_down, g_final=None, ts=512, chunk=256):
    batch, seq, d = x3d.shape
    d_ff = w_down.shape[0]
    assert d_ff % chunk == 0 and seq % ts == 0
    final_norm = g_final is not None
    args = [x3d, state, g.reshape(1, d), w_up, conv_w, conv_b.reshape(1, 2 * d_ff), w_down]
    specs = [pl.BlockSpec((None, ts, d), lambda b, s: (b, s, 0)),
             pl.BlockSpec((None, CONV_W - 1, 2 * d_ff), lambda b, s: (b, 0, 0)),
             _const_spec((1, d)), _const_spec((d, 2 * d_ff)), _const_spec((CONV_W, 2 * d_ff)),
             _const_spec((1, 2 * d_ff)), _const_spec((d_ff, d))]
    if final_norm:
        args.append(g_final.reshape(1, d))
        specs.append(_const_spec((1, d)))
    return pl.pallas_call(
        functools.partial(_ffn_prompt_kernel, ts=ts, d_ff=d_ff, chunk=chunk, final_norm=final_norm),
        grid=(batch, seq // ts),
        in_specs=specs,
        out_specs=[pl.BlockSpec((None, ts, d), lambda b, s: (b, s, 0)),
                   pl.BlockSpec((None, CONV_W - 1, 2 * d_ff), lambda b, s: (b, 0, 0))],
        out_shape=[jax.ShapeDtypeStruct((batch, seq, d), _F32),
                   jax.ShapeDtypeStruct((batch, CONV_W - 1, 2 * d_ff), _F32)],
        scratch_shapes=[pltpu.VMEM((CONV_W - 1, 2 * d_ff), _F32), pltpu.VMEM((ts, d_ff), _BF)],
        compiler_params=_params(2),
        name="conv_ffn_prompt",
    )(*args)


def _ffn_sample_kernel(*refs, d_ff, chunk, final_norm):
    if final_norm:
        x_ref, p2_ref, p1_ref, g_ref, wup_ref, cw_ref, cb_ref, wdn_ref, gf_ref, o_ref, u_ref, act_ref = refs
    else:
        x_ref, p2_ref, p1_ref, g_ref, wup_ref, cw_ref, cb_ref, wdn_ref, o_ref, u_ref, act_ref = refs
        gf_ref = None
    x = x_ref[...]
    hb = _rms(x, g_ref[...]).astype(_BF)

    def up(c0):
        cs = slice(c0, c0 + chunk)
        u = jnp.dot(hb, wup_ref[:, cs], preferred_element_type=_F32)
        u_ref[:, cs] = u
        return u, p1_ref[:, cs], p2_ref[:, cs]

    for c in range(d_ff // chunk):
        cg, cv = c * chunk, d_ff + c * chunk
        act_ref[:, cg:cg + chunk] = _ffn_gate(*up(cg), *up(cv), cw_ref, cb_ref, cg, cv, chunk)
    y = x + jnp.dot(act_ref[...], wdn_ref[...], preferred_element_type=_F32)
    if final_norm:
        y = _rms(y, gf_ref[...])
    o_ref[...] = y


def _ffn_sample(x2d, prev2, prev1, g, w_up, conv_w, conv_b, w_down, g_final=None, chunk=256):
    db, d = x2d.shape
    d_ff = w_down.shape[0]
    final_norm = g_final is not None
    args = [x2d, prev2, prev1, g.reshape(1, d), w_up, conv_w, conv_b.reshape(1, 2 * d_ff), w_down]
    specs = [_const_spec((db, d)), _const_spec((db, 2 * d_ff)), _const_spec((db, 2 * d_ff)),
             _const_spec((1, d)), _const_spec((d, 2 * d_ff)), _const_spec((CONV_W, 2 * d_ff)),
             _const_spec((1, 2 * d_ff)), _const_spec((d_ff, d))]
    if final_norm:
        args.append(g_final.reshape(1, d))
        specs.append(_const_spec((1, d)))
    return pl.pallas_call(
        functools.partial(_ffn_sample_kernel, d_ff=d_ff, chunk=chunk, final_norm=final_norm),
        grid=(1,),
        in_specs=specs,
        out_specs=[pl.BlockSpec((db, d), lambda i: (0, 0)), pl.BlockSpec((db, 2 * d_ff), lambda i: (0, 0))],
        out_shape=[jax.ShapeDtypeStruct((db, d), _F32), jax.ShapeDtypeStruct((db, 2 * d_ff), _F32)],
        scratch_shapes=[pltpu.VMEM((db, d_ff), _BF)],
        compiler_params=_params(1),
        name="conv_ffn_sample",
    )(*args)


def _mla_proj_kernel(x_ref, ck_ref, sk_ref, g_ref, win_ref, gq_ref, gkv_ref, wq_ref, wk_ref, wv_ref,
                     q_ref, k_ref, v_ref, ckv_ref, kr_ref):
    hb = _rms(x_ref[...], g_ref[...]).astype(_BF)
    proj = jnp.dot(hb, win_ref[...], preferred_element_type=_F32)
    c_q = _rms(proj[:, :Q_LORA], gq_ref[...])
    c_kv = _rms(proj[:, Q_LORA:Q_LORA + KV_LORA], gkv_ref[...])
    ckv_ref[...] = c_kv
    off = Q_LORA + KV_LORA
    ck, sk = ck_ref[...], sk_ref[...]
    kr_pad = proj[:, off:off + LANES] * ck + proj[:, off + LANES:off + 2 * LANES] * sk
    kr_ref[...] = kr_pad[:, QK_NOPE:QK_NOPE + QK_ROPE]
    scale = (QK_NOPE + QK_ROPE) ** -0.5 * LOG2E
    lane = lax.broadcasted_iota(jnp.int32, ck.shape, 1)
    nope = (lane < QK_NOPE).astype(_F32)
    ones_col = (lane == V_HEAD).astype(_F32)
    cq_tab = scale * (ck + nope)
    sq_tab = scale * sk
    cqb = c_q.astype(_BF)
    ckvb = c_kv.astype(_BF)
    hw = H_B * LANES
    for h in range(H_B):
        sl = slice(h * LANES, (h + 1) * LANES)
        qa = jnp.dot(cqb, wq_ref[:, sl], preferred_element_type=_F32)
        qs = jnp.dot(cqb, wq_ref[:, hw + h * LANES:hw + (h + 1) * LANES], preferred_element_type=_F32)
        q_ref[h] = (qa * cq_tab + qs * sq_tab).astype(_BF)
        kn = jnp.dot(ckvb, wk_ref[:, sl], preferred_element_type=_F32)
        k_ref[h] = (kn + kr_pad).astype(_BF)
        vv = jnp.dot(ckvb, wv_ref[:, sl], preferred_element_type=_F32)
        v_ref[h] = (vv + ones_col).astype(_BF)


def _mla_proj(x2d, ck_tab, sk_tab, g, w_in_ext, g_q, g_kv, w_q_both, w_k_pad, w_v, n_seq, tm):
    m, d = x2d.shape
    rows = m // n_seq
    tm = min(tm, rows)
    tps = rows // tm
    x_map = lambda s, b: (b * tps + s, 0)
    t_map = lambda s, b: (s, 0)
    return pl.pallas_call(
        _mla_proj_kernel,
        grid=(tps, n_seq),
        in_specs=[pl.BlockSpec((tm, d), x_map), pl.BlockSpec((tm, LANES), t_map), pl.BlockSpec((tm, LANES), t_map),
                  _const_spec((1, d)), _const_spec(w_in_ext.shape), _const_spec((1, Q_LORA)),
                  _const_spec((1, KV_LORA)), _const_spec(w_q_both.shape), _const_spec(w_k_pad.shape),
                  _const_spec(w_v.shape)],
        out_specs=[pl.BlockSpec((H_B, tm, LANES), lambda s, b: (0, b * tps + s, 0)),
                   pl.BlockSpec((H_B, tm, LANES), lambda s, b: (0, b * tps + s, 0)),
                   pl.BlockSpec((H_B, tm, LANES), lambda s, b: (0, b * tps + s, 0)),
                   pl.BlockSpec((tm, KV_LORA), x_map), pl.BlockSpec((tm, QK_ROPE), x_map)],
        out_shape=[jax.ShapeDtypeStruct((H_B, m, LANES), _BF), jax.ShapeDtypeStruct((H_B, m, LANES), _BF),
                   jax.ShapeDtypeStruct((H_B, m, LANES), _BF),
                   jax.ShapeDtypeStruct((m, KV_LORA), _F32), jax.ShapeDtypeStruct((m, QK_ROPE), _F32)],
        compiler_params=_params(2),
        name="mla_proj",
    )(x2d, ck_tab, sk_tab, g.reshape(1, d), w_in_ext, g_q.reshape(1, Q_LORA), g_kv.reshape(1, KV_LORA),
      w_q_both, w_k_pad, w_v)


def _mla_prompt_kernel(q_ref, k_ref, v_ref, o_ref, s_ref, *, seq, tq, ck):
    lo = lax.broadcasted_iota(jnp.int32, (tq, LANES), 1) < V_HEAD
    row = lax.broadcasted_iota(jnp.int32, (tq, tq), 0)
    colk = lax.broadcasted_iota(jnp.int32, (tq, tq), 1)
    causal = colk <= row
    nt = (((1,), (1,)), ((), ()))
    for qi in range(seq // tq):
        q0 = qi * tq
        kv_len = q0 + tq
        chunks = [(c0, min(ck, kv_len - c0)) for c0 in range(0, kv_len, ck)]
        outs = []
        for hh in range(2):
            q = q_ref[hh, q0:q0 + tq, :]
            mx = None
            for c0, cw in chunks:
                s = lax.dot_general(q, k_ref[hh, c0:c0 + cw, :], nt, preferred_element_type=_F32)
                if c0 + cw == kv_len:
                    sd = jnp.where(causal, s[:, cw - tq:], NEG)
                    if cw > tq:
                        s_ref[hh, :, c0:c0 + cw - tq] = s[:, :cw - tq]
                    s_ref[hh, :, kv_len - tq:kv_len] = sd
                    tiles = [s[:, j:j + LANES] for j in range(0, cw - tq, LANES)]
                    tiles += [sd[:, j:j + LANES] for j in range(0, tq, LANES)]
                else:
                    s_ref[hh, :, c0:c0 + cw] = s
                    tiles = [s[:, j:j + LANES] for j in range(0, cw, LANES)]
                for t in tiles:
                    mx = t if mx is None else jnp.maximum(mx, t)
            m = jnp.max(mx, axis=-1, keepdims=True)
            acc = None
            for c0, cw in chunks:
                p = jnp.exp2(s_ref[hh, :, c0:c0 + cw] - m).astype(_BF)
                pv = jnp.dot(p, v_ref[hh, c0:c0 + cw, :], preferred_element_type=_F32)
                acc = pv if acc is None else acc + pv
            outs.append(acc / acc[:, V_HEAD:V_HEAD + 1])
        o_ref[q0:q0 + tq, :] = jnp.where(lo, outs[0], pltpu.roll(outs[1], V_HEAD, 1)).astype(o_ref.dtype)


def _mla_prompt_attn(q, k, v, batch, seq, tq=256, ck=512):
    m = batch * seq
    blk = pl.BlockSpec((2, seq, LANES), lambda b, hp: (hp, b, 0))
    return pl.pallas_call(
        functools.partial(_mla_prompt_kernel, seq=seq, tq=tq, ck=ck),
        grid=(batch, H_B // 2),
        in_specs=[blk, blk, blk],
        out_specs=pl.BlockSpec((seq, LANES), lambda b, hp: (b, hp)),
        out_shape=jax.ShapeDtypeStruct((m, H_B * V_HEAD), _BF),
        scratch_shapes=[pltpu.VMEM((2, tq, seq), _F32)],
        compiler_params=_params(2),
        name="mla_prompt_attn",
    )(q, k, v)


def _mla_absorb_kernel(q_ref, wuk_ref, o_ref):
    for h in range(H_B):
        o_ref[h] = jnp.dot(q_ref[h], wuk_ref[h], preferred_element_type=_F32).astype(o_ref.dtype)


def _mla_absorb(q, w_uk_t_pad):
    db = q.shape[1]
    return pl.pallas_call(
        _mla_absorb_kernel,
        grid=(1,),
        in_specs=[_const_spec(q.shape), _const_spec(w_uk_t_pad.shape)],
        out_specs=pl.BlockSpec((H_B, db, KV_LORA), lambda i: (0, 0, 0)),
        out_shape=jax.ShapeDtypeStruct((H_B, db, KV_LORA), _BF),
        compiler_params=_params(1),
        name="mla_absorb_q",
    )(q, w_uk_t_pad)


def _mla_sample_kernel(pt_ref, qlat_ref, qrope_ref, cnew_ref, krnew_ref, *refs, pages):
    ckv_refs = refs[:pages]
    kr_refs = refs[pages:2 * pages]
    o_ref, m_ref, l_ref, acc_ref = refs[2 * pages:]
    j = pl.program_id(1)

    @pl.when(j == 0)
    def _():
        m_ref[...] = jnp.full(m_ref.shape, -jnp.inf, _F32)
        l_ref[...] = jnp.zeros(l_ref.shape, _F32)
        acc_ref[...] = jnp.zeros(acc_ref.shape, _F32)

    qlat = qlat_ref[...]
    qrope = qrope_ref[...]
    nt = (((1,), (1,)), ((), ()))

    def update(scores, weighted_values):
        mx = scores[0]
        for s in scores[1:]:
            mx = jnp.maximum(mx, s)
        m_prev = m_ref[...]
        m_new = jnp.maximum(m_prev, jnp.max(mx, axis=-1, keepdims=True))
        alpha = jnp.exp2(m_prev - m_new)
        ps = [jnp.exp2(s - m_new) for s in scores]
        psum = ps[0]
        for p in ps[1:]:
            psum = psum + p
        l_ref[...] = alpha * l_ref[...] + jnp.sum(psum, axis=-1, keepdims=True)
        acc_ref[...] = alpha * acc_ref[...] + weighted_values([p.astype(_BF) for p in ps])
        m_ref[...] = m_new

    ckvs = [ckv_refs[t][...].astype(_BF) for t in range(pages)]
    scores = [lax.dot_general(qlat, ckvs[t], nt, preferred_element_type=_F32)
              + jnp.dot(qrope, kr_refs[t][...].astype(_BF), preferred_element_type=_F32)
              for t in range(pages)]

    def past_values(pbs):
        acc = jnp.dot(pbs[0], ckvs[0], preferred_element_type=_F32)
        for t in range(1, pages):
            acc = acc + jnp.dot(pbs[t], ckvs[t], preferred_element_type=_F32)
        return acc

    update(scores, past_values)

    @pl.when(j == pl.num_programs(1) - 1)
    def _():
        cn = cnew_ref[...].astype(_BF).astype(_F32)
        krn = krnew_ref[...].astype(_BF).astype(_F32)
        s = jnp.sum(qlat.astype(_F32) * cn, axis=-1, keepdims=True) \
            + jnp.sum(qrope.astype(_F32) * krn, axis=-1, keepdims=True)
        update([s], lambda pbs: pbs[0].astype(_F32) * cn)
        o_ref[...] = (acc_ref[...] / l_ref[...]).astype(o_ref.dtype)


def _mla_sample_attn(page_table, qlat, qrope, c_new, kr_new, ckv_pool, kr_pool_t, pages=32):
    db, n_pages = page_table.shape
    page = ckv_pool.shape[1]
    pages = min(pages, n_pages)
    assert n_pages % pages == 0
    pt_flat = page_table.reshape(-1)

    def page_map(t):
        return lambda b, j, pt: (pt[b * n_pages + j * pages + t], 0, 0)

    per_b = lambda b, j, pt: (b, 0, 0)
    in_specs = [pl.BlockSpec((None, H_B, KV_LORA), per_b), pl.BlockSpec((None, H_B, QK_ROPE), per_b),
                pl.BlockSpec((None, 1, KV_LORA), per_b), pl.BlockSpec((None, 1, QK_ROPE), per_b)]
    in_specs += [pl.BlockSpec((None, page, KV_LORA), page_map(t)) for t in range(pages)]
    in_specs += [pl.BlockSpec((None, QK_ROPE, page), page_map(t)) for t in range(pages)]
    grid_spec = pltpu.PrefetchScalarGridSpec(
        num_scalar_prefetch=1,
        grid=(db, n_pages // pages),
        in_specs=in_specs,
        out_specs=pl.BlockSpec((None, H_B, KV_LORA), per_b),
        scratch_shapes=[pltpu.VMEM((H_B, 1), _F32), pltpu.VMEM((H_B, 1), _F32), pltpu.VMEM((H_B, KV_LORA), _F32)],
    )
    return pl.pallas_call(
        functools.partial(_mla_sample_kernel, pages=pages),
        grid_spec=grid_spec,
        out_shape=jax.ShapeDtypeStruct((db, H_B, KV_LORA), _BF),
        compiler_params=_params(2),
        name="mla_sample_attn",
    )(pt_flat, qlat, qrope, c_new, kr_new, *([ckv_pool] * pages), *([kr_pool_t] * pages))


def _mla_unabsorb_kernel(o_ref, wuv_ref, out_ref):
    for p in range(H_B // 2):
        y = jnp.dot(o_ref[2 * p], wuv_ref[2 * p], preferred_element_type=_F32) \
            + jnp.dot(o_ref[2 * p + 1], wuv_ref[2 * p + 1], preferred_element_type=_F32)
        out_ref[:, p * LANES:(p + 1) * LANES] = y.astype(out_ref.dtype)


def _mla_unabsorb(o_lat_hm, w_uv_pair):
    db = o_lat_hm.shape[1]
    return pl.pallas_call(
        _mla_unabsorb_kernel,
        grid=(1,),
        in_specs=[_const_spec(o_lat_hm.shape), _const_spec(w_uv_pair.shape)],
        out_specs=pl.BlockSpec((db, H_B * V_HEAD), lambda i: (0, 0)),
        out_shape=jax.ShapeDtypeStruct((db, H_B * V_HEAD), _BF),
        compiler_params=_params(1),
        name="mla_unabsorb_o",
    )(o_lat_hm, w_uv_pair)


def _t5_bucket(dist):
    n = np.asarray(dist)
    max_exact = N_BUCKETS // 2
    large = max_exact + (np.log(np.maximum(n, 1) / max_exact) / np.log(T5_MAX_DISTANCE / max_exact)
                         * (N_BUCKETS - max_exact)).astype(np.int32)
    large = np.minimum(large, N_BUCKETS - 1)
    return np.where(n < max_exact, n, large).astype(np.int32)


def _dsw_biases(table):
    out = []
    for g, (w, d) in enumerate(zip(WINDOWS, DILATIONS)):
        buckets = _t5_bucket(np.arange(w // d + 1) * d)
        out.append(table[buckets][:, g * H_A:(g + 1) * H_A].T.astype(_F32))
    return out


def _dsw_bias_blocks(bias_g):
    rel = np.arange(BLK)[:, None] + BLK - np.arange(2 * BLK)[None, :]
    return bias_g[:, np.clip(rel, 0, BLK)]


def _rope_pad_tables(pos):
    inv = jnp.asarray(ROPE_BASE ** (-np.arange(0, QK_ROPE, 2) / QK_ROPE), dtype=_F32)
    ang = pos.astype(_F32)[:, None] * inv[None, :]
    cos, sin = jnp.cos(ang), jnp.sin(ang)
    n = pos.shape[0]
    zl = jnp.zeros((n, QK_NOPE), _F32)
    zr = jnp.zeros((n, LANES - QK_NOPE - QK_ROPE), _F32)
    return (jnp.concatenate([zl, cos, cos, zr], axis=1), jnp.concatenate([zl, -sin, sin, zr], axis=1))


def _mla_weights(w_in, w_q, w_kv):
    d = w_in.shape[0]
    half = QK_ROPE // 2
    off = Q_LORA + KV_LORA
    kr = w_in[:, off:]
    kr_sw = jnp.concatenate([kr[:, half:], kr[:, :half]], axis=1)
    zl = jnp.zeros((d, QK_NOPE), w_in.dtype)
    zr = jnp.zeros((d, LANES - QK_NOPE - QK_ROPE), w_in.dtype)
    w_in_ext = jnp.concatenate([w_in[:, :off], zl, kr, zr, zl, kr_sw, zr], axis=1).astype(_BF)

    dh = QK_NOPE + QK_ROPE
    wq3 = w_q.reshape(Q_LORA, H_B, dh)
    pad = jnp.zeros((Q_LORA, H_B, LANES - dh), w_q.dtype)
    wq_pad = jnp.concatenate([wq3, pad], axis=2)
    wq_sw = jnp.concatenate([jnp.zeros((Q_LORA, H_B, QK_NOPE), w_q.dtype), wq3[:, :, QK_NOPE + half:],
                             wq3[:, :, QK_NOPE:QK_NOPE + half], pad], axis=2)
    w_q_both = jnp.concatenate([wq_pad.reshape(Q_LORA, H_B * LANES), wq_sw.reshape(Q_LORA, H_B * LANES)],
                               axis=1).astype(_BF)

    wkv3 = w_kv.reshape(KV_LORA, H_B, QK_NOPE + V_HEAD)
    w_uk, w_uv = wkv3[:, :, :QK_NOPE], wkv3[:, :, QK_NOPE:]
    w_k_pad = jnp.concatenate([w_uk, jnp.zeros((KV_LORA, H_B, LANES - QK_NOPE), w_kv.dtype)], axis=2)
    w_k_pad = w_k_pad.reshape(KV_LORA, H_B * LANES).astype(_BF)
    w_v = jnp.concatenate([w_uv, jnp.zeros((KV_LORA, H_B, LANES - V_HEAD), w_kv.dtype)], axis=2)
    w_v = w_v.reshape(KV_LORA, H_B * LANES).astype(_BF)
    w_uk_t = jnp.transpose(w_uk, (1, 2, 0))
    w_uk_t_pad = jnp.concatenate([w_uk_t, jnp.zeros((H_B, LANES - QK_NOPE, KV_LORA), w_kv.dtype)], axis=1)
    w_uv_h = jnp.transpose(w_uv, (1, 0, 2))
    zero = jnp.zeros_like(w_uv_h)
    even = jnp.concatenate([w_uv_h, zero], axis=2)
    odd = jnp.concatenate([zero, w_uv_h], axis=2)
    is_even = (jnp.arange(H_B) % 2 == 0)[:, None, None]
    w_uv_pair = jnp.where(is_even, even, odd)
    return w_in_ext, w_q_both, w_k_pad, w_v, w_uk_t_pad.astype(_BF), w_uv_pair.astype(_BF)


def kernel(x_prompt, x_sample, cache_dsw_g0, cache_dsw_g1, cache_dsw_g2, cache_mla_ckv, cache_mla_kr,
           state_ffn_conv, page_table, rel_bias_table, norm_mix, norm_ffn, norm_final, w_qkv_dsw, w_o_dsw,
           w_in_mla, g_q_mla, g_kv_mla, w_q_mla, w_kv_mla, w_o_mla, w_up_ffn, conv_w_ffn, conv_b_ffn, w_down_ffn):
    batch, seq, d = x_prompt.shape
    db, t_new, _ = x_sample.shape
    assert t_new == 1
    depth = norm_mix.shape[0]
    d_ff = w_down_ffn.shape[1]
    past_len = page_table.shape[1] * cache_mla_ckv.shape[2]
    m = batch * seq
    dsw_caches = (cache_dsw_g0, cache_dsw_g1, cache_dsw_g2)
    biases = _dsw_biases(rel_bias_table)
    bias_blocks = [_dsw_bias_blocks(bg) for bg in biases]

    xp = x_prompt.reshape(m, d)
    xs = x_sample.reshape(db, d)
    dsw_new_p = [[] for _ in range(N_GROUPS)]
    dsw_new_s = [[] for _ in range(N_GROUPS)]
    ckv_p, ckv_s, kr_p, kr_s, conv_p, conv_s = [], [], [], [], [], []
    zero_state = jnp.zeros((batch, CONV_W - 1, 2 * d_ff), _F32)

    for i in range(depth):
        if i % 2 == 0:
            a = i // 2
            w_qkv = w_qkv_dsw[a].astype(_BF)
            w_o = w_o_dsw[a].astype(_BF)
            *qkv_groups, c0, c1, c2 = _qkv_prompt(xp.reshape(batch, seq, d), norm_mix[i], w_qkv)
            for g, c in enumerate((c0, c1, c2)):
                c = c.reshape(batch, 2, H_A, DH_A, min(WINDOWS[g], seq))
                dsw_new_p[g].append(jnp.transpose(c, (0, 4, 1, 2, 3)))
            outs, lses = zip(*[_dsw_prompt_attn(qkv_groups[g], bias_blocks[g], g) for g in range(N_GROUPS)])
            xp = _dsw_merge_slab(outs, lses, w_o, xp.reshape(batch, seq, d)).reshape(m, d)
            qkv_s = _linear(xs, w_qkv, norm_g=norm_mix[i], name="dsw_qkv_sample")
            w_kv_t = jnp.transpose(w_qkv.reshape(d, N_GROUPS, 3, D_ATT)[:, :, 1:], (1, 2, 3, 0))
            kvt_s = _linear_t(xs, norm_mix[i], w_kv_t.reshape(N_GROUPS * 2 * D_ATT, d), "dsw_kv_sample_t")
            kvt_s = kvt_s.reshape(N_GROUPS, 2 * D_ATT, db)
            outs, lses = [], []
            for g in range(N_GROUPS):
                cache_t = jnp.transpose(dsw_caches[g][a], (0, 2, 3, 4, 1)).reshape(db, 2 * D_ATT, WINDOWS[g])
                o, lse, new = _dsw_sample(qkv_s, kvt_s, cache_t, biases[g], g)
                outs.append(o)
                lses.append(lse)
                new = new.reshape(db, 2, H_A, DH_A, WINDOWS[g])
                dsw_new_s[g].append(jnp.transpose(new, (0, 4, 1, 2, 3)))
            xs = _dsw_merge(outs, lses, w_o, xs)
        else:
            b = i // 2
            w_in_ext, w_q_both, w_k_pad, w_v, w_uk_t_pad, w_uv_pair = _mla_weights(
                w_in_mla[b], w_q_mla[b], w_kv_mla[b])
            w_o = w_o_mla[b].astype(_BF)
            ck_tab, sk_tab = _rope_pad_tables(jnp.arange(seq))
            q, k, v, ckv, kr = _mla_proj(xp, ck_tab, sk_tab, norm_mix[i], w_in_ext, g_q_mla[b], g_kv_mla[b],
                                         w_q_both, w_k_pad, w_v, n_seq=batch, tm=512)
            ckv_p.append(ckv.reshape(batch, seq, KV_LORA))
            kr_p.append(kr.reshape(batch, seq, QK_ROPE))
            o = _mla_prompt_attn(q, k, v, batch, seq)
            xp = _linear(o, w_o, residual=xp, name="mla_out_prompt")
            ck_s, sk_s = _rope_pad_tables(jnp.full((db,), past_len, jnp.int32))
            q, _, _, ckv, kr = _mla_proj(xs, ck_s, sk_s, norm_mix[i], w_in_ext, g_q_mla[b], g_kv_mla[b],
                                         w_q_both, w_k_pad, w_v, n_seq=1, tm=db)
            ckv_s.append(ckv.reshape(db, 1, KV_LORA))
            kr_s.append(kr.reshape(db, 1, QK_ROPE))
            qlat = jnp.transpose(_mla_absorb(q, w_uk_t_pad), (1, 0, 2))
            qrope = jnp.transpose(q[:, :, QK_NOPE:QK_NOPE + QK_ROPE], (1, 0, 2))
            o_lat = _mla_sample_attn(page_table, qlat, qrope, ckv.reshape(db, 1, KV_LORA),
                                     kr.reshape(db, 1, QK_ROPE), cache_mla_ckv[b],
                                     jnp.transpose(cache_mla_kr[b], (0, 2, 1)))
            o = _mla_unabsorb(jnp.transpose(o_lat, (1, 0, 2)), w_uv_pair)
            xs = _linear(o, w_o, residual=xs, name="mla_out_sample")

        g_final = norm_final if i == depth - 1 else None
        w_up = w_up_ffn[i].astype(_BF)
        w_dn = w_down_ffn[i].astype(_BF)
        yp, cp = _ffn_prompt(xp.reshape(batch, seq, d), zero_state, norm_ffn[i], w_up, conv_w_ffn[i],
                             conv_b_ffn[i], w_dn, g_final)
        xp = yp.reshape(m, d)
        conv_p.append(cp)
        prev2, prev1 = state_ffn_conv[i, :, 0], state_ffn_conv[i, :, 1]
        xs, u_s = _ffn_sample(xs, prev2, prev1, norm_ffn[i], w_up, conv_w_ffn[i], conv_b_ffn[i], w_dn, g_final)
        conv_s.append(jnp.stack([prev1, u_s], axis=1))

    def stack(per_layer):
        return per_layer[0][None] if len(per_layer) == 1 else jnp.stack(per_layer)

    y_prompt = xp.reshape(batch, seq, d)
    y_sample = xs.reshape(db, 1, d)
    return (y_prompt, y_sample,
            stack(dsw_new_p[0]), stack(dsw_new_s[0]), stack(dsw_new_p[1]), stack(dsw_new_s[1]),
            stack(dsw_new_p[2]), stack(dsw_new_s[2]),
            stack(ckv_p), stack(ckv_s), stack(kr_p), stack(kr_s), stack(conv_p), stack(conv_s))
```

```python
import functools

import numpy as np
import jax
import jax.numpy as jnp
from jax import lax
from jax.experimental import pallas as pl
from jax.experimental.pallas import tpu as pltpu

WINDOWS = (128, 512, 2048)
DILATIONS = (1, 4, 16)
N_GROUPS = 3
H_A = 8
DH_A = 64
D_ATT = H_A * DH_A
BLK = 128
N_BUCKETS = 32
T5_MAX_DISTANCE = 2048
H_B = 16
Q_LORA = 384
KV_LORA = 256
QK_NOPE = 64
QK_ROPE = 32
V_HEAD = 64
ROPE_BASE = 10000.0
CONV_W = 3
EPS = 1e-6
NEG = -1e30
LOG2E = 1.4426950408889634

LANES = 128
VMEM_LIMIT = 56 * 1024 * 1024

_BF = jnp.bfloat16
_F32 = jnp.float32


def _params(n_axes, vmem=VMEM_LIMIT):
    return pltpu.CompilerParams(dimension_semantics=("arbitrary",) * n_axes, vmem_limit_bytes=vmem)


def _const_spec(shape):
    nd = len(shape)
    return pl.BlockSpec(shape, lambda *_: (0,) * nd, pipeline_mode=pl.Buffered(1))


def _rms(x, g):
    return x * lax.rsqrt(jnp.mean(x * x, axis=-1, keepdims=True) + EPS) * g


def _linear_kernel(*refs, has_norm, has_res, n_chunk):
    it = iter(refs)
    x_ref = next(it)
    g_ref = next(it) if has_norm else None
    w_ref = next(it)
    r_ref = next(it) if has_res else None
    o_ref = next(it)
    x = x_ref[...]
    if has_norm:
        x = _rms(x.astype(_F32), g_ref[...])
    xb = x.astype(_BF)
    n = w_ref.shape[1]
    for c0 in range(0, n, n_chunk):
        y = jnp.dot(xb, w_ref[:, c0:c0 + n_chunk], preferred_element_type=_F32)
        if has_res:
            y = y + r_ref[:, c0:c0 + n_chunk]
        o_ref[:, c0:c0 + n_chunk] = y.astype(o_ref.dtype)


def _linear(x, w, *, norm_g=None, residual=None, out_dtype=_F32, tm=512, name="linear"):
    m, k = x.shape
    n = w.shape[1]
    tm = min(tm, m)
    assert m % tm == 0
    n_chunk = 512 if n % 512 == 0 else n
    args, specs = [x], [pl.BlockSpec((tm, k), lambda i: (i, 0))]
    if norm_g is not None:
        args.append(norm_g.reshape(1, k))
        specs.append(_const_spec((1, k)))
    args.append(w)
    specs.append(_const_spec((k, n)))
    if residual is not None:
        args.append(residual)
        specs.append(pl.BlockSpec((tm, n), lambda i: (i, 0)))
    return pl.pallas_call(
        functools.partial(_linear_kernel, has_norm=norm_g is not None, has_res=residual is not None,
                          n_chunk=n_chunk),
        grid=(m // tm,),
        in_specs=specs,
        out_specs=pl.BlockSpec((tm, n), lambda i: (i, 0)),
        out_shape=jax.ShapeDtypeStruct((m, n), out_dtype),
        compiler_params=_params(1),
        name=name,
    )(*args)


_PAIRS = H_A // 2
_QKV_SLOTS = 3


def _qkv_prompt_kernel(x_ref, g_ref, w_ref, q0_ref, q1_ref, q2_ref, c0_ref, c1_ref, c2_ref, scr_ref, *, tm,
                       tiles_per_seq, keeps):
    last_tile = pl.program_id(1) == tiles_per_seq - 1
    hb = _rms(x_ref[...], g_ref[...]).astype(_BF)
    out_refs = (q0_ref, q1_ref, q2_ref)
    cache_refs = (c0_ref, c1_ref, c2_ref)

    def project(k):
        for half in range(2):
            c0 = k * D_ATT + half * 2 * LANES
            y = jnp.dot(hb, w_ref[:, c0:c0 + 2 * LANES], preferred_element_type=_F32)
            if k % 3 == 0:
                y = y * (DH_A ** -0.5)
            for p in range(2):
                scr_ref[k % _QKV_SLOTS, 2 * half + p] = y[:, p * LANES:(p + 1) * LANES]

    def emit(k):
        g, part = divmod(k, 3)
        dil = DILATIONS[g]
        rows = tm // dil
        oref, cref, keep = out_refs[g], cache_refs[g], keeps[g]
        for r in range(dil):
            for p in range(_PAIRS):
                c0 = part * D_ATT + p * LANES
                src = pl.ds(0, tm) if dil == 1 else pl.ds(r, rows, stride=dil)
                oref[r, :, c0:c0 + LANES] = scr_ref[k % _QKV_SLOTS, p, src, :].astype(_BF)
        if part > 0:
            def kept_positions():
                for p in range(_PAIRS):
                    c0 = (part - 1) * D_ATT + p * LANES
                    cref[c0:c0 + LANES, :] = scr_ref[k % _QKV_SLOTS, p, tm - min(keep, tm):tm, :].T
            if keep == tm * tiles_per_seq:
                kept_positions()
            else:
                pl.when(last_tile)(kept_positions)

    n_proj = 3 * N_GROUPS
    project(0)
    for k in range(1, n_proj + 1):
        if k < n_proj:
            project(k)
        emit(k - 1)


def _qkv_prompt(x3d, g, w, tm=512):
    batch, seq, d = x3d.shape
    n = w.shape[1]
    tps = seq // tm
    keeps = tuple(min(wd, seq) for wd in WINDOWS)
    out_shapes, out_specs = [], []
    for dil in DILATIONS:
        assert tm % (16 * dil) == 0
        out_shapes.append(jax.ShapeDtypeStruct((batch, dil, seq // dil, 3 * D_ATT), _BF))
        out_specs.append(pl.BlockSpec((None, dil, tm // dil, 3 * D_ATT), lambda b, s: (b, 0, s, 0)))
    for keep in keeps:
        assert keep == seq or keep <= tm
        out_shapes.append(jax.ShapeDtypeStruct((batch, 2 * D_ATT, keep), _F32))
        if keep == seq:
            out_specs.append(pl.BlockSpec((None, 2 * D_ATT, tm), lambda b, s: (b, 0, s)))
        else:
            out_specs.append(pl.BlockSpec((None, 2 * D_ATT, keep), lambda b, s: (b, 0, 0)))
    return pl.pallas_call(
        functools.partial(_qkv_prompt_kernel, tm=tm, tiles_per_seq=tps, keeps=keeps),
        grid=(batch, tps),
        in_specs=[pl.BlockSpec((None, tm, d), lambda b, s: (b, s, 0)), _const_spec((1, d)), _const_spec((d, n))],
        out_specs=out_specs,
        out_shape=out_shapes,
        scratch_shapes=[pltpu.VMEM((_QKV_SLOTS, _PAIRS, tm, LANES), _F32)],
        compiler_params=_params(2),
        name="dsw_qkv_prompt",
    )(x3d, g.reshape(1, d), w)


def _dsw_prompt_kernel(qkv_ref, bias_ref, o_ref, lse_ref, *, dil, nb):
    nk = 2 * BLK if nb > 1 else BLK
    qi = lax.broadcasted_iota(jnp.int32, (BLK, nk), 0)
    ki = lax.broadcasted_iota(jnp.int32, (BLK, nk), 1)
    rel = qi + (nk - BLK) - ki
    band = (rel >= 0) & (rel <= BLK)
    lo = lax.broadcasted_iota(jnp.int32, (BLK, LANES), 1) < DH_A

    def block(idx, carry):
        r = idx // nb
        n = idx % nb
        q0 = pl.multiple_of(n * BLK, BLK)
        if nb > 1:
            p0 = pl.multiple_of(jnp.maximum(n - 1, 0) * BLK, BLK)
            valid = band & jnp.logical_or(ki >= BLK, n > 0)
        else:
            valid = band
        start = q0 * dil + r
        rows = pl.ds(start, BLK) if dil == 1 else pl.ds(start, BLK, stride=dil)
        vps, scores = [], []
        for pair in range(_PAIRS):
            cq, ck, cv = pair * LANES, D_ATT + pair * LANES, 2 * D_ATT + pair * LANES
            qp = qkv_ref[r, pl.ds(q0, BLK), cq:cq + LANES]
            kp = qkv_ref[r, pl.ds(q0, BLK), ck:ck + LANES]
            vp = qkv_ref[r, pl.ds(q0, BLK), cv:cv + LANES]
            if nb > 1:
                kp = jnp.concatenate([qkv_ref[r, pl.ds(p0, BLK), ck:ck + LANES], kp], axis=0)
                vp = jnp.concatenate([qkv_ref[r, pl.ds(p0, BLK), cv:cv + LANES], vp], axis=0)
            vps.append(vp)
            for hh in range(2):
                sel = lo if hh == 0 else jnp.logical_not(lo)
                qm = jnp.where(sel, qp, jnp.zeros_like(qp))
                s = lax.dot_general(qm, kp, _NT, preferred_element_type=_F32)
                scores.append(jnp.where(valid, s + bias_ref[pair * 2 + hh], NEG))
        ms = [jnp.max(s, axis=-1, keepdims=True) for s in scores]
        ps = [jnp.exp(s - m) for s, m in zip(scores, ms)]
        ls = [jnp.sum(p, axis=-1, keepdims=True) for p in ps]
        pvs = [jnp.dot(p.astype(_BF), vps[h // 2], preferred_element_type=_F32) for h, p in enumerate(ps)]
        for pair in range(_PAIRS):
            h0, h1 = 2 * pair, 2 * pair + 1
            o_ref[pair, rows, :] = jnp.where(lo, pvs[h0] / ls[h0], pvs[h1] / ls[h1])
            lse_ref[pair, rows, :] = jnp.where(lo, jnp.broadcast_to(ms[h0] + jnp.log(ls[h0]), (BLK, LANES)),
                                               jnp.broadcast_to(ms[h1] + jnp.log(ls[h1]), (BLK, LANES)))
        return carry

    lax.fori_loop(0, dil * nb, block, 0)


def _dsw_prompt_attn(qkv_g, bias_full, g):
    batch, dil, sub, width = qkv_g.shape
    seq = dil * sub
    assert seq % WINDOWS[g] == 0 and dil == DILATIONS[g]
    nb = sub // BLK
    if nb == 1:
        bias_full = bias_full[:, :, BLK:]
    out = jax.ShapeDtypeStruct((batch, _PAIRS, seq, LANES), _F32)
    out_spec = pl.BlockSpec((None, _PAIRS, seq, LANES), lambda b: (b, 0, 0, 0))
    return pl.pallas_call(
        functools.partial(_dsw_prompt_kernel, dil=dil, nb=nb),
        grid=(batch,),
        in_specs=[pl.BlockSpec((None, dil, sub, width), lambda b: (b, 0, 0, 0)), _const_spec(bias_full.shape)],
        out_specs=[out_spec, out_spec],
        out_shape=[out, out],
        compiler_params=_params(1),
        name=f"dsw_prompt_attn_g{g}",
    )(qkv_g, bias_full)


def _merge_groups(os, ls):
    la, lb, lc = ls
    mx = jnp.maximum(jnp.maximum(la, lb), lc)
    ea, eb, ec = jnp.exp(la - mx), jnp.exp(lb - mx), jnp.exp(lc - mx)
    return (ea * os[0] + eb * os[1] + ec * os[2]) / (ea + eb + ec)


def _dsw_merge_slab_kernel(o0_ref, o1_ref, o2_ref, l0_ref, l1_ref, l2_ref, w_ref, x_ref, out_ref):
    o_refs, l_refs = (o0_ref, o1_ref, o2_ref), (l0_ref, l1_ref, l2_ref)
    slabs = [_merge_groups([r[p] for r in o_refs], [r[p] for r in l_refs]).astype(_BF) for p in range(_PAIRS)]
    o = jnp.concatenate(slabs, axis=1)
    out_ref[...] = x_ref[...] + jnp.dot(o, w_ref[...], preferred_element_type=_F32)


def _dsw_merge_slab(outs, lses, w_o, x3d, tm=512):
    batch, seq, d = x3d.shape
    spec = pl.BlockSpec((None, _PAIRS, tm, LANES), lambda b, s: (b, 0, s, 0))
    xspec = pl.BlockSpec((None, tm, d), lambda b, s: (b, s, 0))
    return pl.pallas_call(
        _dsw_merge_slab_kernel,
        grid=(batch, seq // tm),
        in_specs=[spec] * 6 + [_const_spec((D_ATT, d)), xspec],
        out_specs=xspec,
        out_shape=jax.ShapeDtypeStruct((batch, seq, d), _F32),
        compiler_params=_params(2),
        name="dsw_merge_out_prompt",
    )(*outs, *lses, w_o, x3d)


def _dsw_merge_kernel(o0_ref, o1_ref, o2_ref, l0_ref, l1_ref, l2_ref, w_ref, x_ref, out_ref):
    o = _merge_groups([o0_ref[...], o1_ref[...], o2_ref[...]], [l0_ref[...], l1_ref[...], l2_ref[...]])
    out_ref[...] = x_ref[...] + jnp.dot(o.astype(_BF), w_ref[...], preferred_element_type=_F32)


def _dsw_merge(outs, lses, w_o, x2d, tm=512):
    m, d = x2d.shape
    tm = min(tm, m)
    spec = pl.BlockSpec((tm, D_ATT), lambda i: (i, 0))
    return pl.pallas_call(
        _dsw_merge_kernel,
        grid=(m // tm,),
        in_specs=[spec] * 6 + [_const_spec((D_ATT, d)), pl.BlockSpec((tm, d), lambda i: (i, 0))],
        out_specs=pl.BlockSpec((tm, d), lambda i: (i, 0)),
        out_shape=jax.ShapeDtypeStruct((m, d), _F32),
        compiler_params=_params(1),
        name="dsw_merge_out",
    )(*outs, *lses, w_o, x2d)


_NT = (((1,), (1,)), ((), ()))


def _linear_t_kernel(x_ref, g_ref, wt_ref, o_ref, *, n_chunk):
    hb = _rms(x_ref[...], g_ref[...]).astype(_BF)
    for c0 in range(0, wt_ref.shape[0], n_chunk):
        o_ref[c0:c0 + n_chunk, :] = lax.dot_general(wt_ref[c0:c0 + n_chunk, :], hb, _NT,
                                                    preferred_element_type=_F32)


def _linear_t(x, norm_g, wt, name):
    m, k = x.shape
    n = wt.shape[0]
    return pl.pallas_call(
        functools.partial(_linear_t_kernel, n_chunk=512),
        grid=(1,),
        in_specs=[_const_spec((m, k)), _const_spec((1, k)), _const_spec((n, k))],
        out_specs=pl.BlockSpec((n, m), lambda i: (0, 0)),
        out_shape=jax.ShapeDtypeStruct((n, m), _F32),
        compiler_params=_params(1),
        name=name,
    )(x, norm_g.reshape(1, k), wt)


def _dsw_sample_kernel(qkv_ref, kvt_ref, cache_ref, bias_ref, b0_ref, o_ref, lse_ref, new_ref, *, g, window,
                       dil, rows):
    b = pl.program_id(0)
    base = g * 3 * D_ATT
    q = qkv_ref[:, base:base + D_ATT] * (DH_A ** -0.5)
    knew = qkv_ref[:, base + D_ATT:base + 2 * D_ATT]
    vnew = qkv_ref[:, base + 2 * D_ATT:base + 3 * D_ATT]
    head = lax.broadcasted_iota(jnp.int32, (H_A, D_ATT), 0)
    lane_head = lax.broadcasted_iota(jnp.int32, (H_A, D_ATT), 1) // DH_A
    hm = head == lane_head
    qrows = jnp.where(hm, jnp.broadcast_to(q, (H_A, D_ATT)), 0.0).astype(_BF)
    s = jnp.dot(qrows, cache_ref[0:D_ATT, :].astype(_BF), preferred_element_type=_F32)
    pos = lax.broadcasted_iota(jnp.int32, (H_A, window), 1)
    s = jnp.where((pos & (dil - 1)) == 0, s + bias_ref[...], NEG)
    prod = q.astype(_BF).astype(_F32) * knew.astype(_BF).astype(_F32)
    s_new = jnp.sum(jnp.where(hm, jnp.broadcast_to(prod, (H_A, D_ATT)), 0.0), axis=-1, keepdims=True) \
        + b0_ref[...]
    m = jnp.maximum(jnp.max(s, axis=-1, keepdims=True), s_new)
    pr = jnp.exp(s - m)
    p_new = jnp.exp(s_new - m)
    l = jnp.sum(pr, axis=-1, keepdims=True) + p_new
    o8 = lax.dot_general(pr.astype(_BF), cache_ref[D_ATT:2 * D_ATT, :].astype(_BF), _NT,
                         preferred_element_type=_F32)
    o8 = (o8 + p_new.astype(_BF).astype(_F32) * vnew.astype(_BF).astype(_F32)) / l
    lse8 = jnp.broadcast_to(m + jnp.log(l), (H_A, D_ATT))
    o_ref[...] = jnp.sum(jnp.where(hm, o8, 0.0), axis=0, keepdims=True)
    lse_ref[...] = jnp.sum(jnp.where(hm, lse8, 0.0), axis=0, keepdims=True)
    mine = lax.broadcasted_iota(jnp.int32, (rows, kvt_ref.shape[1]), 1) == b
    last = lax.broadcasted_iota(jnp.int32, (rows, LANES), 1) == LANES - 1
    for r0 in range(0, 2 * D_ATT, rows):
        rolled = pltpu.roll(cache_ref[r0:r0 + rows, :], window - 1, 1)
        col = jnp.sum(jnp.where(mine, kvt_ref[r0:r0 + rows, :], 0.0), axis=1, keepdims=True)
        if window > LANES:
            new_ref[r0:r0 + rows, :window - LANES] = rolled[:, :window - LANES]
        new_ref[r0:r0 + rows, window - LANES:] = jnp.where(last, col, rolled[:, window - LANES:])


def _dsw_sample(qkv_s, kvt_s, cache_t, bias_g, g):
    db = qkv_s.shape[0]
    window, dil = WINDOWS[g], DILATIONS[g]
    assert cache_t.shape[2] == window and dil & (dil - 1) == 0
    width = qkv_s.shape[-1]
    w = np.arange(window)
    back = np.where(w % dil == 0, (window - w) // dil, 0)
    bias_pos = bias_g[:, back]
    b0 = bias_g[:, 0:1]
    blk = (None, 2 * D_ATT, window)
    o, lse, new = pl.pallas_call(
        functools.partial(_dsw_sample_kernel, g=g, window=window, dil=dil, rows=128),
        grid=(db,),
        in_specs=[pl.BlockSpec((None, 1, width), lambda b: (b, 0, 0)),
                  pl.BlockSpec((None, 2 * D_ATT, db), lambda b: (g, 0, 0)),
                  pl.BlockSpec(blk, lambda b: (b, 0, 0)),
                  _const_spec((H_A, window)), _const_spec((H_A, 1))],
        out_specs=[pl.BlockSpec((None, 1, D_ATT), lambda b: (b, 0, 0)),
                   pl.BlockSpec((None, 1, D_ATT), lambda b: (b, 0, 0)),
                   pl.BlockSpec(blk, lambda b: (b, 0, 0))],
        out_shape=[jax.ShapeDtypeStruct((db, 1, D_ATT), _F32), jax.ShapeDtypeStruct((db, 1, D_ATT), _F32),
                   jax.ShapeDtypeStruct((db, 2 * D_ATT, window), _F32)],
        compiler_params=_params(1),
        name=f"dsw_sample_g{g}",
    )(qkv_s.reshape(db, 1, width), kvt_s, cache_t, bias_pos, b0)
    return o.reshape(db, D_ATT), lse.reshape(db, D_ATT), new


def _ffn_gate(ug, ug1, ug2, uv, uv1, uv2, cw_ref, cb_ref, cg, cv, chunk):
    def conv(u, u1, u2, c0):
        cs = slice(c0, c0 + chunk)
        return cb_ref[:, cs] + cw_ref[0:1, cs] * u2 + cw_ref[1:2, cs] * u1 + cw_ref[2:3, cs] * u
    yg = conv(ug, ug1, ug2, cg)
    yv = conv(uv, uv1, uv2, cv)
    return (yg * (1.0 / (1.0 + jnp.exp(-yg))) * yv).astype(_BF)


def _ffn_prompt_kernel(*refs, ts, d_ff, chunk, final_norm):
    if final_norm:
        x_ref, st_ref, g_ref, wup_ref, cw_ref, cb_ref, wdn_ref, gf_ref, o_ref, ns_ref, carry_ref, act_ref = refs
    else:
        x_ref, st_ref, g_ref, wup_ref, cw_ref, cb_ref, wdn_ref, o_ref, ns_ref, carry_ref, act_ref = refs
        gf_ref = None
    s = pl.program_id(1)

    @pl.when(s == 0)
    def _():
        carry_ref[...] = st_ref[...]

    x = x_ref[...]
    hb = _rms(x, g_ref[...]).astype(_BF)
    row = lax.broadcasted_iota(jnp.int32, (ts, chunk), 0)
    first, second = row == 0, row == 1

    def up(c0):
        return jnp.dot(hb, wup_ref[:, c0:c0 + chunk], preferred_element_type=_F32)

    def taps(u, c0):
        p2 = carry_ref[0:1, c0:c0 + chunk]
        p1 = carry_ref[1:2, c0:c0 + chunk]
        u1 = jnp.where(first, p1, pltpu.roll(u, 1, 0))
        u2 = jnp.where(first, p2, jnp.where(second, p1, pltpu.roll(u, 2, 0)))
        carry_ref[:, c0:c0 + chunk] = u[ts - 2:ts, :]
        return u, u1, u2

    n_chunks = d_ff // chunk
    nxt = (up(0), up(d_ff))
    for c in range(n_chunks):
        cg, cv = c * chunk, d_ff + c * chunk
        ug, uv = nxt
        if c + 1 < n_chunks:
            nxt = (up(cg + chunk), up(cv + chunk))
        act_ref[:, cg:cg + chunk] = _ffn_gate(*taps(ug, cg), *taps(uv, cv), cw_ref, cb_ref, cg, cv, chunk)
    y = x + jnp.dot(act_ref[...], wdn_ref[...], preferred_element_type=_F32)
    if final_norm:
        y = _rms(y, gf_ref[...])
    o_ref[...] = y

    @pl.when(s == pl.num_programs(1) - 1)
    def _():
        ns_ref[...] = carry_ref[...]


def _ffn_prompt(x3d, state, g, w_up, conv_w, conv_b, w_down, g_final=None, ts=512, chunk=256):
    batch, seq, d = x3d.shape
    d_ff = w_down.shape[0]
    assert d_ff % chunk == 0 and seq % ts == 0
    final_norm = g_final is not None
    args = [x3d, state, g.reshape(1, d), w_up, conv_w, conv_b.reshape(1, 2 * d_ff), w_down]
    specs = [pl.BlockSpec((None, ts, d), lambda b, s: (b, s, 0)),
             pl.BlockSpec((None, CONV_W - 1, 2 * d_ff), lambda b, s: (b, 0, 0)),
             _const_spec((1, d)), _const_spec((d, 2 * d_ff)), _const_spec((CONV_W, 2 * d_ff)),
             _const_spec((1, 2 * d_ff)), _const_spec((d_ff, d))]
    if final_norm:
        args.append(g_final.reshape(1, d))
        specs.append(_const_spec((1, d)))
    return pl.pallas_call(
        functools.partial(_ffn_prompt_kernel, ts=ts, d_ff=d_ff, chunk=chunk, final_norm=final_norm),
        grid=(batch, seq // ts),
        in_specs=specs,
        out_specs=[pl.BlockSpec((None, ts, d), lambda b, s: (b, s, 0)),
                   pl.BlockSpec((None, CONV_W - 1, 2 * d_ff), lambda b, s: (b, 0, 0))],
        out_shape=[jax.ShapeDtypeStruct((batch, seq, d), _F32),
                   jax.ShapeDtypeStruct((batch, CONV_W - 1, 2 * d_ff), _F32)],
        scratch_shapes=[pltpu.VMEM((CONV_W - 1, 2 * d_ff), _F32), pltpu.VMEM((ts, d_ff), _BF)],
        compiler_params=_params(2),
        name="conv_ffn_prompt",
    )(*args)


def _ffn_sample_kernel(*refs, d_ff, chunk, final_norm):
    if final_norm:
        x_ref, p2_ref, p1_ref, g_ref, wup_ref, cw_ref, cb_ref, wdn_ref, gf_ref, o_ref, u_ref, act_ref = refs
    else:
        x_ref, p2_ref, p1_ref, g_ref, wup_ref, cw_ref, cb_ref, wdn_ref, o_ref, u_ref, act_ref = refs
        gf_ref = None
    x = x_ref[...]
    hb = _rms(x, g_ref[...]).astype(_BF)

    def up(c0):
        cs = slice(c0, c0 + chunk)
        u = jnp.dot(hb, wup_ref[:, cs], preferred_element_type=_F32)
        u_ref[:, cs] = u
        return u, p1_ref[:, cs], p2_ref[:, cs]

    for c in range(d_ff // chunk):
        cg, cv = c * chunk, d_ff + c * chunk
        act_ref[:, cg:cg + chunk] = _ffn_gate(*up(cg), *up(cv), cw_ref, cb_ref, cg, cv, chunk)
    y = x + jnp.dot(act_ref[...], wdn_ref[...], preferred_element_type=_F32)
    if final_norm:
        y = _rms(y, gf_ref[...])
    o_ref[...] = y


def _ffn_sample(x2d, prev2, prev1, g, w_up, conv_w, conv_b, w_down, g_final=None, chunk=256):
    db, d = x2d.shape
    d_ff = w_down.shape[0]
    final_norm = g_final is not None
    args = [x2d, prev2, prev1, g.reshape(1, d), w_up, conv_w, conv_b.reshape(1, 2 * d_ff), w_down]
    specs = [_const_spec((db, d)), _const_spec((db, 2 * d_ff)), _const_spec((db, 2 * d_ff)),
             _const_spec((1, d)), _const_spec((d, 2 * d_ff)), _const_spec((CONV_W, 2 * d_ff)),
             _const_spec((1, 2 * d_ff)), _const_spec((d_ff, d))]
    if final_norm:
        args.append(g_final.reshape(1, d))
        specs.append(_const_spec((1, d)))
    return pl.pallas_call(
        functools.partial(_ffn_sample_kernel, d_ff=d_ff, chunk=chunk, final_norm=final_norm),
        grid=(1,),
        in_specs=specs,
        out_specs=[pl.BlockSpec((db, d), lambda i: (0, 0)), pl.BlockSpec((db, 2 * d_ff), lambda i: (0, 0))],
        out_shape=[jax.ShapeDtypeStruct((db, d), _F32), jax.ShapeDtypeStruct((db, 2 * d_ff), _F32)],
        scratch_shapes=[pltpu.VMEM((db, d_ff), _BF)],
        compiler_params=_params(1),
        name="conv_ffn_sample",
    )(*args)


def _mla_proj_kernel(x_ref, ck_ref, sk_ref, g_ref, win_ref, gq_ref, gkv_ref, wq_ref, wk_ref, wv_ref,
                     q_ref, k_ref, v_ref, ckv_ref, kr_ref):
    hb = _rms(x_ref[...], g_ref[...]).astype(_BF)
    proj = jnp.dot(hb, win_ref[...], preferred_element_type=_F32)
    c_q = _rms(proj[:, :Q_LORA], gq_ref[...])
    c_kv = _rms(proj[:, Q_LORA:Q_LORA + KV_LORA], gkv_ref[...])
    ckv_ref[...] = c_kv
    off = Q_LORA + KV_LORA
    ck, sk = ck_ref[...], sk_ref[...]
    kr_pad = proj[:, off:off + LANES] * ck + proj[:, off + LANES:off + 2 * LANES] * sk
    kr_ref[...] = kr_pad[:, QK_NOPE:QK_NOPE + QK_ROPE]
    scale = (QK_NOPE + QK_ROPE) ** -0.5 * LOG2E
    lane = lax.broadcasted_iota(jnp.int32, ck.shape, 1)
    nope = (lane < QK_NOPE).astype(_F32)
    ones_col = (lane == V_HEAD).astype(_F32)
    cq_tab = scale * (ck + nope)
    sq_tab = scale * sk
    cqb = c_q.astype(_BF)
    ckvb = c_kv.astype(_BF)
    hw = H_B * LANES
    for h in range(H_B):
        sl = slice(h * LANES, (h + 1) * LANES)
        qa = jnp.dot(cqb, wq_ref[:, sl], preferred_element_type=_F32)
        qs = jnp.dot(cqb, wq_ref[:, hw + h * LANES:hw + (h + 1) * LANES], preferred_element_type=_F32)
        q_ref[h] = (qa * cq_tab + qs * sq_tab).astype(_BF)
        kn = jnp.dot(ckvb, wk_ref[:, sl], preferred_element_type=_F32)
        k_ref[h] = (kn + kr_pad).astype(_BF)
        vv = jnp.dot(ckvb, wv_ref[:, sl], preferred_element_type=_F32)
        v_ref[h] = (vv + ones_col).astype(_BF)


def _mla_proj(x2d, ck_tab, sk_tab, g, w_in_ext, g_q, g_kv, w_q_both, w_k_pad, w_v, n_seq, tm):
    m, d = x2d.shape
    rows = m // n_seq
    tm = min(tm, rows)
    tps = rows // tm
    x_map = lambda s, b: (b * tps + s, 0)
    t_map = lambda s, b: (s, 0)
    return pl.pallas_call(
        _mla_proj_kernel,
        grid=(tps, n_seq),
        in_specs=[pl.BlockSpec((tm, d), x_map), pl.BlockSpec((tm, LANES), t_map), pl.BlockSpec((tm, LANES), t_map),
                  _const_spec((1, d)), _const_spec(w_in_ext.shape), _const_spec((1, Q_LORA)),
                  _const_spec((1, KV_LORA)), _const_spec(w_q_both.shape), _const_spec(w_k_pad.shape),
                  _const_spec(w_v.shape)],
        out_specs=[pl.BlockSpec((H_B, tm, LANES), lambda s, b: (0, b * tps + s, 0)),
                   pl.BlockSpec((H_B, tm, LANES), lambda s, b: (0, b * tps + s, 0)),
                   pl.BlockSpec((H_B, tm, LANES), lambda s, b: (0, b * tps + s, 0)),
                   pl.BlockSpec((tm, KV_LORA), x_map), pl.BlockSpec((tm, QK_ROPE), x_map)],
        out_shape=[jax.ShapeDtypeStruct((H_B, m, LANES), _BF), jax.ShapeDtypeStruct((H_B, m, LANES), _BF),
                   jax.ShapeDtypeStruct((H_B, m, LANES), _BF),
                   jax.ShapeDtypeStruct((m, KV_LORA), _F32), jax.ShapeDtypeStruct((m, QK_ROPE), _F32)],
        compiler_params=_params(2),
        name="mla_proj",
    )(x2d, ck_tab, sk_tab, g.reshape(1, d), w_in_ext, g_q.reshape(1, Q_LORA), g_kv.reshape(1, KV_LORA),
      w_q_both, w_k_pad, w_v)


def _mla_prompt_kernel(q_ref, k_ref, v_ref, o_ref, s_ref, *, seq, tq, ck):
    lo = lax.broadcasted_iota(jnp.int32, (tq, LANES), 1) < V_HEAD
    row = lax.broadcasted_iota(jnp.int32, (tq, tq), 0)
    colk = lax.broadcasted_iota(jnp.int32, (tq, tq), 1)
    causal = colk <= row
    units = [(qi, hh) for qi in range(seq // tq) for hh in range(2)]
    state = [dict(mx=None, m=None, acc=None) for _ in units]
    outs = {}

    def chunks_of(qi):
        kv_len = (qi + 1) * tq
        return [(c0, min(ck, kv_len - c0)) for c0 in range(0, kv_len, ck)]

    def score_tasks(u):
        (qi, hh), st, slot = units[u], state[u], u % 2
        q0, kv_len = qi * tq, (qi + 1) * tq

        def task(c0, cw):
            s = lax.dot_general(q_ref[hh, q0:q0 + tq, :], k_ref[hh, c0:c0 + cw, :], _NT,
                                preferred_element_type=_F32)
            if c0 + cw == kv_len:
                sd = jnp.where(causal, s[:, cw - tq:], NEG)
                if cw > tq:
                    s_ref[slot, :, c0:c0 + cw - tq] = s[:, :cw - tq]
                s_ref[slot, :, kv_len - tq:kv_len] = sd
                tiles = [s[:, j:j + LANES] for j in range(0, cw - tq, LANES)]
                tiles += [sd[:, j:j + LANES] for j in range(0, tq, LANES)]
            else:
                s_ref[slot, :, c0:c0 + cw] = s
                tiles = [s[:, j:j + LANES] for j in range(0, cw, LANES)]
            for t in tiles:
                st["mx"] = t if st["mx"] is None else jnp.maximum(st["mx"], t)

        return [functools.partial(task, c0, cw) for c0, cw in chunks_of(qi)]

    def value_tasks(u):
        (qi, hh), st, slot = units[u], state[u], u % 2
        q0 = qi * tq

        def task(c0, cw, first, last):
            if first:
                st["m"] = jnp.max(st["mx"], axis=-1, keepdims=True)
            p = jnp.exp2(s_ref[slot, :, c0:c0 + cw] - st["m"]).astype(_BF)
            pv = jnp.dot(p, v_ref[hh, c0:c0 + cw, :], preferred_element_type=_F32)
            st["acc"] = pv if st["acc"] is None else st["acc"] + pv
            if last:
                outs[hh] = st["acc"] / st["acc"][:, V_HEAD:V_HEAD + 1]
                if hh == 1:
                    o_ref[q0:q0 + tq, :] = jnp.where(lo, outs[0], pltpu.roll(outs[1], V_HEAD, 1)).astype(o_ref.dtype)

        ch = chunks_of(qi)
        return [functools.partial(task, c0, cw, i == 0, i == len(ch) - 1) for i, (c0, cw) in enumerate(ch)]

    pending = []
    for u in range(len(units) + 1):
        scoring = score_tasks(u) if u < len(units) else []
        for i in range(max(len(scoring), len(pending))):
            if i < len(scoring):
                scoring[i]()
            if i < len(pending):
                pending[i]()
        pending = value_tasks(u) if u < len(units) else []


def _mla_prompt_attn(q, k, v, batch, seq, tq=256, ck=512):
    m = batch * seq
    blk = pl.BlockSpec((2, seq, LANES), lambda b, hp: (hp, b, 0))
    return pl.pallas_call(
        functools.partial(_mla_prompt_kernel, seq=seq, tq=tq, ck=ck),
        grid=(batch, H_B // 2),
        in_specs=[blk, blk, blk],
        out_specs=pl.BlockSpec((seq, LANES), lambda b, hp: (b, hp)),
        out_shape=jax.ShapeDtypeStruct((m, H_B * V_HEAD), _BF),
        scratch_shapes=[pltpu.VMEM((2, tq, seq), _F32)],
        compiler_params=_params(2),
        name="mla_prompt_attn",
    )(q, k, v)


def _mla_absorb_kernel(q_ref, wuk_ref, o_ref):
    for h in range(H_B):
        o_ref[h] = jnp.dot(q_ref[h], wuk_ref[h], preferred_element_type=_F32).astype(o_ref.dtype)


def _mla_absorb(q, w_uk_t_pad):
    db = q.shape[1]
    return pl.pallas_call(
        _mla_absorb_kernel,
        grid=(1,),
        in_specs=[_const_spec(q.shape), _const_spec(w_uk_t_pad.shape)],
        out_specs=pl.BlockSpec((H_B, db, KV_LORA), lambda i: (0, 0, 0)),
        out_shape=jax.ShapeDtypeStruct((H_B, db, KV_LORA), _BF),
        compiler_params=_params(1),
        name="mla_absorb_q",
    )(q, w_uk_t_pad)


def _mla_sample_kernel(pt_ref, qlat_ref, qrope_ref, cnew_ref, krnew_ref, ckv_hbm, kr_hbm, o_ref,
                       ckv_buf, kr_buf, sems, *, pages, n_pages):
    b = pl.program_id(0)
    n_chunks = n_pages // pages

    def copies(page_of, slot):
        out = []
        for t in range(pages):
            page = page_of(t)
            out.append(pltpu.make_async_copy(ckv_hbm.at[page], ckv_buf.at[slot, t], sems.at[0, slot]))
            out.append(pltpu.make_async_copy(kr_hbm.at[page], kr_buf.at[slot, t], sems.at[1, slot]))
        return out

    def start(seq, chunk, slot):
        for cp in copies(lambda t: pt_ref[seq * n_pages + chunk * pages + t], slot):
            cp.start()

    def wait(slot):
        for cp in copies(lambda t: 0, slot):
            cp.wait()

    @pl.when(b == 0)
    def _():
        start(0, 0, 0)

    qlat = qlat_ref[...]
    qrope = qrope_ref[...]

    def update(carry, s, weighted_values):
        m_prev, l_prev, acc = carry
        m_new = jnp.maximum(m_prev, jnp.max(s, axis=-1, keepdims=True))
        alpha = jnp.exp2(m_prev - m_new)
        p = jnp.exp2(s - m_new)
        return (m_new, alpha * l_prev + jnp.sum(p, axis=-1, keepdims=True),
                alpha * acc + weighted_values(p.astype(_BF)))

    carry = (jnp.full((H_B, 1), -jnp.inf, _F32), jnp.zeros((H_B, 1), _F32), jnp.zeros((H_B, KV_LORA), _F32))
    for c in range(n_chunks):
        slot = c % 2
        if c + 1 < n_chunks:
            start(b, c + 1, 1 - slot)
        else:
            @pl.when(b + 1 < pl.num_programs(0))
            def _():
                start(b + 1, 0, 1 - slot)
        wait(slot)
        ckv = ckv_buf[slot].reshape(pages * ckv_buf.shape[2], KV_LORA).astype(_BF)
        s_rope = jnp.concatenate(
            [jnp.dot(qrope, kr_buf[slot, t].astype(_BF), preferred_element_type=_F32) for t in range(pages)],
            axis=1)
        s = lax.dot_general(qlat, ckv, _NT, preferred_element_type=_F32) + s_rope
        carry = update(carry, s, lambda pb, ckv=ckv: jnp.dot(pb, ckv, preferred_element_type=_F32))

    cn = cnew_ref[...].astype(_BF).astype(_F32)
    krn = krnew_ref[...].astype(_BF).astype(_F32)
    s = jnp.sum(qlat.astype(_F32) * cn, axis=-1, keepdims=True) \
        + jnp.sum(qrope.astype(_F32) * krn, axis=-1, keepdims=True)
    _, l, acc = update(carry, s, lambda pb: pb.astype(_F32) * cn)
    o_ref[...] = (acc / l).astype(o_ref.dtype)


def _mla_sample_attn(page_table, qlat, qrope, c_new, kr_new, ckv_pool, kr_pool_t, pages=32):
    db, n_pages = page_table.shape
    page = ckv_pool.shape[1]
    pages = min(pages, n_pages // 2)
    assert n_pages % (2 * pages) == 0
    per_b = lambda b, pt: (b, 0, 0)
    grid_spec = pltpu.PrefetchScalarGridSpec(
        num_scalar_prefetch=1,
        grid=(db,),
        in_specs=[pl.BlockSpec((None, H_B, KV_LORA), per_b), pl.BlockSpec((None, H_B, QK_ROPE), per_b),
                  pl.BlockSpec((None, 1, KV_LORA), per_b), pl.BlockSpec((None, 1, QK_ROPE), per_b),
                  pl.BlockSpec(memory_space=pl.ANY), pl.BlockSpec(memory_space=pl.ANY)],
        out_specs=pl.BlockSpec((None, H_B, KV_LORA), per_b),
        scratch_shapes=[pltpu.VMEM((2, pages, page, KV_LORA), _F32), pltpu.VMEM((2, pages, QK_ROPE, page), _F32),
                        pltpu.SemaphoreType.DMA((2, 2))],
    )
    return pl.pallas_call(
        functools.partial(_mla_sample_kernel, pages=pages, n_pages=n_pages),
        grid_spec=grid_spec,
        out_shape=jax.ShapeDtypeStruct((db, H_B, KV_LORA), _BF),
        compiler_params=_params(1),
        name="mla_sample_attn",
    )(page_table.reshape(-1), qlat, qrope, c_new, kr_new, ckv_pool, kr_pool_t)


def _mla_unabsorb_kernel(o_ref, wuv_ref, out_ref):
    for p in range(H_B // 2):
        y = jnp.dot(o_ref[2 * p], wuv_ref[2 * p], preferred_element_type=_F32) \
            + jnp.dot(o_ref[2 * p + 1], wuv_ref[2 * p + 1], preferred_element_type=_F32)
        out_ref[:, p * LANES:(p + 1) * LANES] = y.astype(out_ref.dtype)


def _mla_unabsorb(o_lat_hm, w_uv_pair):
    db = o_lat_hm.shape[1]
    return pl.pallas_call(
        _mla_unabsorb_kernel,
        grid=(1,),
        in_specs=[_const_spec(o_lat_hm.shape), _const_spec(w_uv_pair.shape)],
        out_specs=pl.BlockSpec((db, H_B * V_HEAD), lambda i: (0, 0)),
        out_shape=jax.ShapeDtypeStruct((db, H_B * V_HEAD), _BF),
        compiler_params=_params(1),
        name="mla_unabsorb_o",
    )(o_lat_hm, w_uv_pair)


def _t5_bucket(dist):
    n = np.asarray(dist)
    max_exact = N_BUCKETS // 2
    large = max_exact + (np.log(np.maximum(n, 1) / max_exact) / np.log(T5_MAX_DISTANCE / max_exact)
                         * (N_BUCKETS - max_exact)).astype(np.int32)
    large = np.minimum(large, N_BUCKETS - 1)
    return np.where(n < max_exact, n, large).astype(np.int32)


def _dsw_biases(table):
    out = []
    for g, (w, d) in enumerate(zip(WINDOWS, DILATIONS)):
        buckets = _t5_bucket(np.arange(w // d + 1) * d)
        out.append(table[buckets][:, g * H_A:(g + 1) * H_A].T.astype(_F32))
    return out


def _dsw_bias_blocks(bias_g):
    n = 3 * BLK
    diag = bias_g[:, np.clip(2 * BLK - 1 - np.arange(n), 0, BLK)]
    skew = jnp.tile(diag, (1, BLK))[:, :BLK * (n - 1)].reshape(bias_g.shape[0], BLK, n - 1)
    return skew[:, :, BLK - 1:3 * BLK - 1]


def _rope_pad_tables(pos):
    inv = jnp.asarray(ROPE_BASE ** (-np.arange(0, QK_ROPE, 2) / QK_ROPE), dtype=_F32)
    ang = pos.astype(_F32)[:, None] * inv[None, :]
    cos, sin = jnp.cos(ang), jnp.sin(ang)
    n = pos.shape[0]
    zl = jnp.zeros((n, QK_NOPE), _F32)
    zr = jnp.zeros((n, LANES - QK_NOPE - QK_ROPE), _F32)
    return (jnp.concatenate([zl, cos, cos, zr], axis=1), jnp.concatenate([zl, -sin, sin, zr], axis=1))


def _mla_weights(w_in, w_q, w_kv):
    d = w_in.shape[0]
    half = QK_ROPE // 2
    off = Q_LORA + KV_LORA
    kr = w_in[:, off:]
    kr_sw = jnp.concatenate([kr[:, half:], kr[:, :half]], axis=1)
    zl = jnp.zeros((d, QK_NOPE), w_in.dtype)
    zr = jnp.zeros((d, LANES - QK_NOPE - QK_ROPE), w_in.dtype)
    w_in_ext = jnp.concatenate([w_in[:, :off], zl, kr, zr, zl, kr_sw, zr], axis=1).astype(_BF)

    dh = QK_NOPE + QK_ROPE
    wq3 = w_q.reshape(Q_LORA, H_B, dh)
    pad = jnp.zeros((Q_LORA, H_B, LANES - dh), w_q.dtype)
    wq_pad = jnp.concatenate([wq3, pad], axis=2)
    wq_sw = jnp.concatenate([jnp.zeros((Q_LORA, H_B, QK_NOPE), w_q.dtype), wq3[:, :, QK_NOPE + half:],
                             wq3[:, :, QK_NOPE:QK_NOPE + half], pad], axis=2)
    w_q_both = jnp.concatenate([wq_pad.reshape(Q_LORA, H_B * LANES), wq_sw.reshape(Q_LORA, H_B * LANES)],
                               axis=1).astype(_BF)

    wkv3 = w_kv.reshape(KV_LORA, H_B, QK_NOPE + V_HEAD)
    w_uk, w_uv = wkv3[:, :, :QK_NOPE], wkv3[:, :, QK_NOPE:]
    w_k_pad = jnp.concatenate([w_uk, jnp.zeros((KV_LORA, H_B, LANES - QK_NOPE), w_kv.dtype)], axis=2)
    w_k_pad = w_k_pad.reshape(KV_LORA, H_B * LANES).astype(_BF)
    w_v = jnp.concatenate([w_uv, jnp.zeros((KV_LORA, H_B, LANES - V_HEAD), w_kv.dtype)], axis=2)
    w_v = w_v.reshape(KV_LORA, H_B * LANES).astype(_BF)
    w_uk_t = jnp.transpose(w_uk, (1, 2, 0))
    w_uk_t_pad = jnp.concatenate([w_uk_t, jnp.zeros((H_B, LANES - QK_NOPE, KV_LORA), w_kv.dtype)], axis=1)
    w_uv_h = jnp.transpose(w_uv, (1, 0, 2))
    zero = jnp.zeros_like(w_uv_h)
    even = jnp.concatenate([w_uv_h, zero], axis=2)
    odd = jnp.concatenate([zero, w_uv_h], axis=2)
    is_even = (jnp.arange(H_B) % 2 == 0)[:, None, None]
    w_uv_pair = jnp.where(is_even, even, odd)
    return w_in_ext, w_q_both, w_k_pad, w_v, w_uk_t_pad.astype(_BF), w_uv_pair.astype(_BF)


def kernel(x_prompt, x_sample, cache_dsw_g0, cache_dsw_g1, cache_dsw_g2, cache_mla_ckv, cache_mla_kr,
           state_ffn_conv, page_table, rel_bias_table, norm_mix, norm_ffn, norm_final, w_qkv_dsw, w_o_dsw,
           w_in_mla, g_q_mla, g_kv_mla, w_q_mla, w_kv_mla, w_o_mla, w_up_ffn, conv_w_ffn, conv_b_ffn, w_down_ffn):
    batch, seq, d = x_prompt.shape
    db, t_new, _ = x_sample.shape
    assert t_new == 1
    depth = norm_mix.shape[0]
    d_ff = w_down_ffn.shape[1]
    past_len = page_table.shape[1] * cache_mla_ckv.shape[2]
    m = batch * seq
    dsw_caches = (cache_dsw_g0, cache_dsw_g1, cache_dsw_g2)
    biases = _dsw_biases(rel_bias_table)
    bias_blocks = [_dsw_bias_blocks(bg) for bg in biases]

    xp = x_prompt.reshape(m, d)
    xs = x_sample.reshape(db, d)
    dsw_new_p = [[] for _ in range(N_GROUPS)]
    dsw_new_s = [[] for _ in range(N_GROUPS)]
    ckv_p, ckv_s, kr_p, kr_s, conv_p, conv_s = [], [], [], [], [], []
    zero_state = jnp.zeros((batch, CONV_W - 1, 2 * d_ff), _F32)

    for i in range(depth):
        if i % 2 == 0:
            a = i // 2
            w_qkv = w_qkv_dsw[a].astype(_BF)
            w_o = w_o_dsw[a].astype(_BF)
            *qkv_groups, c0, c1, c2 = _qkv_prompt(xp.reshape(batch, seq, d), norm_mix[i], w_qkv)
            for g, c in enumerate((c0, c1, c2)):
                c = c.reshape(batch, 2, H_A, DH_A, min(WINDOWS[g], seq))
                dsw_new_p[g].append(jnp.transpose(c, (0, 4, 1, 2, 3)))
            outs, lses = zip(*[_dsw_prompt_attn(qkv_groups[g], bias_blocks[g], g) for g in range(N_GROUPS)])
            xp = _dsw_merge_slab(outs, lses, w_o, xp.reshape(batch, seq, d)).reshape(m, d)
            qkv_s = _linear(xs, w_qkv, norm_g=norm_mix[i], name="dsw_qkv_sample")
            w_kv_t = jnp.transpose(w_qkv.reshape(d, N_GROUPS, 3, D_ATT)[:, :, 1:], (1, 2, 3, 0))
            kvt_s = _linear_t(xs, norm_mix[i], w_kv_t.reshape(N_GROUPS * 2 * D_ATT, d), "dsw_kv_sample_t")
            kvt_s = kvt_s.reshape(N_GROUPS, 2 * D_ATT, db)
            outs, lses = [], []
            for g in range(N_GROUPS):
                cache_t = jnp.transpose(dsw_caches[g][a], (0, 2, 3, 4, 1)).reshape(db, 2 * D_ATT, WINDOWS[g])
                o, lse, new = _dsw_sample(qkv_s, kvt_s, cache_t, biases[g], g)
                outs.append(o)
                lses.append(lse)
                new = new.reshape(db, 2, H_A, DH_A, WINDOWS[g])
                dsw_new_s[g].append(jnp.transpose(new, (0, 4, 1, 2, 3)))
            xs = _dsw_merge(outs, lses, w_o, xs)
        else:
            b = i // 2
            w_in_ext, w_q_both, w_k_pad, w_v, w_uk_t_pad, w_uv_pair = _mla_weights(
                w_in_mla[b], w_q_mla[b], w_kv_mla[b])
            w_o = w_o_mla[b].astype(_BF)
            ck_tab, sk_tab = _rope_pad_tables(jnp.arange(seq))
            q, k, v, ckv, kr = _mla_proj(xp, ck_tab, sk_tab, norm_mix[i], w_in_ext, g_q_mla[b], g_kv_mla[b],
                                         w_q_both, w_k_pad, w_v, n_seq=batch, tm=512)
            ckv_p.append(ckv.reshape(batch, seq, KV_LORA))
            kr_p.append(kr.reshape(batch, seq, QK_ROPE))
            o = _mla_prompt_attn(q, k, v, batch, seq)
            xp = _linear(o, w_o, residual=xp, name="mla_out_prompt")
            ck_s, sk_s = _rope_pad_tables(jnp.full((db,), past_len, jnp.int32))
            q, _, _, ckv, kr = _mla_proj(xs, ck_s, sk_s, norm_mix[i], w_in_ext, g_q_mla[b], g_kv_mla[b],
                                         w_q_both, w_k_pad, w_v, n_seq=1, tm=db)
            ckv_s.append(ckv.reshape(db, 1, KV_LORA))
            kr_s.append(kr.reshape(db, 1, QK_ROPE))
            qlat = jnp.transpose(_mla_absorb(q, w_uk_t_pad), (1, 0, 2))
            qrope = jnp.transpose(q[:, :, QK_NOPE:QK_NOPE + QK_ROPE], (1, 0, 2))
            o_lat = _mla_sample_attn(page_table, qlat, qrope, ckv.reshape(db, 1, KV_LORA),
                                     kr.reshape(db, 1, QK_ROPE), cache_mla_ckv[b],
                                     jnp.transpose(cache_mla_kr[b], (0, 2, 1)))
            o = _mla_unabsorb(jnp.transpose(o_lat, (1, 0, 2)), w_uv_pair)
            xs = _linear(o, w_o, residual=xs, name="mla_out_sample")

        g_final = norm_final if i == depth - 1 else None
        w_up = w_up_ffn[i].astype(_BF)
        w_dn = w_down_ffn[i].astype(_BF)
        yp, cp = _ffn_prompt(xp.reshape(batch, seq, d), zero_state, norm_ffn[i], w_up, conv_w_ffn[i],
                             conv_b_ffn[i], w_dn, g_final)
        xp = yp.reshape(m, d)
        conv_p.append(cp)
        prev2, prev1 = state_ffn_conv[i, :, 0], state_ffn_conv[i, :, 1]
        xs, u_s = _ffn_sample(xs, prev2, prev1, norm_ffn[i], w_up, conv_w_ffn[i], conv_b_ffn[i], w_dn, g_final)
        conv_s.append(jnp.stack([prev1, u_s], axis=1))

    def stack(per_layer):
        return per_layer[0][None] if len(per_layer) == 1 else jnp.stack(per_layer)

    y_prompt = xp.reshape(batch, seq, d)
    y_sample = xs.reshape(db, 1, d)
    return (y_prompt, y_sample,
            stack(dsw_new_p[0]), stack(dsw_new_s[0]), stack(dsw_new_p[1]), stack(dsw_new_s[1]),
            stack(dsw_new_p[2]), stack(dsw_new_s[2]),
            stack(ckv_p), stack(ckv_s), stack(kr_p), stack(kr_s), stack(conv_p), stack(conv_s))
```

```python
import functools

import numpy as np
import jax
import jax.numpy as jnp
from jax import lax
from jax.experimental import pallas as pl
from jax.experimental.pallas import tpu as pltpu

WINDOWS = (128, 512, 2048)
DILATIONS = (1, 4, 16)
N_GROUPS = 3
H_A = 8
DH_A = 64
D_ATT = H_A * DH_A
BLK = 128
N_BUCKETS = 32
T5_MAX_DISTANCE = 2048
H_B = 16
Q_LORA = 384
KV_LORA = 256
QK_NOPE = 64
QK_ROPE = 32
V_HEAD = 64
ROPE_BASE = 10000.0
CONV_W = 3
EPS = 1e-6
NEG = -1e30
LOG2E = 1.4426950408889634

LANES = 128
VMEM_LIMIT = 56 * 1024 * 1024

_BF = jnp.bfloat16
_F32 = jnp.float32


def _params(n_axes, vmem=VMEM_LIMIT):
    return pltpu.CompilerParams(dimension_semantics=("arbitrary",) * n_axes, vmem_limit_bytes=vmem)


def _const_spec(shape):
    nd = len(shape)
    return pl.BlockSpec(shape, lambda *_: (0,) * nd, pipeline_mode=pl.Buffered(1))


def _rms(x, g):
    return x * lax.rsqrt(jnp.mean(x * x, axis=-1, keepdims=True) + EPS) * g


def _linear_kernel(*refs, has_norm, has_res, n_chunk):
    it = iter(refs)
    x_ref = next(it)
    g_ref = next(it) if has_norm else None
    w_ref = next(it)
    r_ref = next(it) if has_res else None
    o_ref = next(it)
    x = x_ref[...]
    if has_norm:
        x = _rms(x.astype(_F32), g_ref[...])
    xb = x.astype(_BF)
    n = w_ref.shape[1]
    for c0 in range(0, n, n_chunk):
        y = jnp.dot(xb, w_ref[:, c0:c0 + n_chunk], preferred_element_type=_F32)
        if has_res:
            y = y + r_ref[:, c0:c0 + n_chunk]
        o_ref[:, c0:c0 + n_chunk] = y.astype(o_ref.dtype)


def _linear(x, w, *, norm_g=None, residual=None, out_dtype=_F32, tm=512, name="linear"):
    m, k = x.shape
    n = w.shape[1]
    tm = min(tm, m)
    assert m % tm == 0
    n_chunk = 512 if n % 512 == 0 else n
    args, specs = [x], [pl.BlockSpec((tm, k), lambda i: (i, 0))]
    if norm_g is not None:
        args.append(norm_g.reshape(1, k))
        specs.append(_const_spec((1, k)))
    args.append(w)
    specs.append(_const_spec((k, n)))
    if residual is not None:
        args.append(residual)
        specs.append(pl.BlockSpec((tm, n), lambda i: (i, 0)))
    return pl.pallas_call(
        functools.partial(_linear_kernel, has_norm=norm_g is not None, has_res=residual is not None,
                          n_chunk=n_chunk),
        grid=(m // tm,),
        in_specs=specs,
        out_specs=pl.BlockSpec((tm, n), lambda i: (i, 0)),
        out_shape=jax.ShapeDtypeStruct((m, n), out_dtype),
        compiler_params=_params(1),
        name=name,
    )(*args)


_PAIRS = H_A // 2
_QKV_SLOTS = 3


def _qkv_prompt_kernel(x_ref, g_ref, w_ref, q0_ref, q1_ref, q2_ref, c0_ref, c1_ref, c2_ref, scr_ref, hb_ref,
                       scr2_ref, *, tm, tiles_per_seq, keeps):
    last_tile = pl.program_id(1) == tiles_per_seq - 1
    hb_ref[...] = _rms(x_ref[...], g_ref[...]).astype(_BF)
    out_refs = (q0_ref, q1_ref, q2_ref)
    cache_refs = (c0_ref, c1_ref, c2_ref)

    def project(k):
        for half in range(2):
            c0 = k * D_ATT + half * 2 * LANES
            y = jnp.dot(hb_ref[...], w_ref[:, c0:c0 + 2 * LANES], preferred_element_type=_F32)
            if k % 3 == 0:
                y = y * (DH_A ** -0.5)
            for p in range(2):
                scr_ref[k % _QKV_SLOTS, 2 * half + p] = y[:, p * LANES:(p + 1) * LANES]

    def emit(k):
        g, part = divmod(k, 3)
        dil = DILATIONS[g]
        rows = tm // dil
        oref, cref, keep = out_refs[g], cache_refs[g], keeps[g]
        staged, fine = scr_ref.at[k % _QKV_SLOTS], dil
        if dil > 4:
            fine, grp = dil // 4, tm // 4
            for r4 in range(4):
                for p in range(_PAIRS):
                    scr2_ref[p, r4 * grp:(r4 + 1) * grp, :] = staged[p, pl.ds(r4, grp, stride=4), :]
            staged = scr2_ref
        for r in range(dil):
            for p in range(_PAIRS):
                c0 = part * D_ATT + p * LANES
                if dil == 1:
                    src = pl.ds(0, tm)
                elif dil > 4:
                    src = pl.ds((r % 4) * grp + r // 4, rows, stride=fine)
                else:
                    src = pl.ds(r, rows, stride=dil)
                oref[r, :, c0:c0 + LANES] = staged[p, src, :].astype(_BF)
        if part > 0:
            def kept_positions():
                for p in range(_PAIRS):
                    c0 = (part - 1) * D_ATT + p * LANES
                    cref[c0:c0 + LANES, :] = scr_ref[k % _QKV_SLOTS, p, tm - min(keep, tm):tm, :].T
            if keep == tm * tiles_per_seq:
                kept_positions()
            else:
                pl.when(last_tile)(kept_positions)

    n_proj = 3 * N_GROUPS
    project(0)
    for k in range(1, n_proj + 1):
        if k < n_proj:
            project(k)
        emit(k - 1)


def _qkv_prompt(x3d, g, w, tm=512):
    batch, seq, d = x3d.shape
    n = w.shape[1]
    tps = seq // tm
    keeps = tuple(min(wd, seq) for wd in WINDOWS)
    out_shapes, out_specs = [], []
    for dil in DILATIONS:
        assert tm % (16 * dil) == 0
        out_shapes.append(jax.ShapeDtypeStruct((batch, dil, seq // dil, 3 * D_ATT), _BF))
        out_specs.append(pl.BlockSpec((None, dil, tm // dil, 3 * D_ATT), lambda b, s: (b, 0, s, 0)))
    for keep in keeps:
        assert keep == seq or keep <= tm
        out_shapes.append(jax.ShapeDtypeStruct((batch, 2 * D_ATT, keep), _F32))
        if keep == seq:
            out_specs.append(pl.BlockSpec((None, 2 * D_ATT, tm), lambda b, s: (b, 0, s)))
        else:
            out_specs.append(pl.BlockSpec((None, 2 * D_ATT, keep), lambda b, s: (b, 0, 0)))
    return pl.pallas_call(
        functools.partial(_qkv_prompt_kernel, tm=tm, tiles_per_seq=tps, keeps=keeps),
        grid=(batch, tps),
        in_specs=[pl.BlockSpec((None, tm, d), lambda b, s: (b, s, 0)), _const_spec((1, d)), _const_spec((d, n))],
        out_specs=out_specs,
        out_shape=out_shapes,
        scratch_shapes=[pltpu.VMEM((_QKV_SLOTS, _PAIRS, tm, LANES), _F32), pltpu.VMEM((tm, d), _BF),
                        pltpu.VMEM((_PAIRS, tm, LANES), _F32)],
        compiler_params=_params(2),
        name="dsw_qkv_prompt",
    )(x3d, g.reshape(1, d), w)


def _dsw_prompt_kernel(qkv_ref, bias_ref, o_ref, lse_ref, *, dil, nb):
    nk = 2 * BLK if nb > 1 else BLK
    qi = lax.broadcasted_iota(jnp.int32, (BLK, nk), 0)
    ki = lax.broadcasted_iota(jnp.int32, (BLK, nk), 1)
    rel = qi + (nk - BLK) - ki
    band = (rel >= 0) & (rel <= BLK)
    lo = lax.broadcasted_iota(jnp.int32, (BLK, LANES), 1) < DH_A

    def block(idx, carry):
        r = idx // nb
        n = idx % nb
        q0 = pl.multiple_of(n * BLK, BLK)
        if nb > 1:
            p0 = pl.multiple_of(jnp.maximum(n - 1, 0) * BLK, BLK)
            valid = band & jnp.logical_or(ki >= BLK, n > 0)
        else:
            valid = band
        start = q0 * dil + r
        rows = pl.ds(start, BLK) if dil == 1 else pl.ds(start, BLK, stride=dil)
        vps, scores = [], []
        for pair in range(_PAIRS):
            cq, ck, cv = pair * LANES, D_ATT + pair * LANES, 2 * D_ATT + pair * LANES
            qp = qkv_ref[r, pl.ds(q0, BLK), cq:cq + LANES]
            kp = qkv_ref[r, pl.ds(q0, BLK), ck:ck + LANES]
            vp = qkv_ref[r, pl.ds(q0, BLK), cv:cv + LANES]
            if nb > 1:
                kp = jnp.concatenate([qkv_ref[r, pl.ds(p0, BLK), ck:ck + LANES], kp], axis=0)
                vp = jnp.concatenate([qkv_ref[r, pl.ds(p0, BLK), cv:cv + LANES], vp], axis=0)
            vps.append(vp)
            for hh in range(2):
                sel = lo if hh == 0 else jnp.logical_not(lo)
                qm = jnp.where(sel, qp, jnp.zeros_like(qp))
                s = lax.dot_general(qm, kp, _NT, preferred_element_type=_F32)
                scores.append(jnp.where(valid, s + bias_ref[pair * 2 + hh], NEG))
        ms = [jnp.max(s, axis=-1, keepdims=True) for s in scores]
        ps = [jnp.exp(s - m) for s, m in zip(scores, ms)]
        ls = [jnp.sum(p, axis=-1, keepdims=True) for p in ps]
        pvs = [jnp.dot(p.astype(_BF), vps[h // 2], preferred_element_type=_F32) for h, p in enumerate(ps)]
        for pair in range(_PAIRS):
            h0, h1 = 2 * pair, 2 * pair + 1
            o_ref[pair, rows, :] = jnp.where(lo, pvs[h0] / ls[h0], pvs[h1] / ls[h1])
            lse_ref[pair, rows, :] = jnp.where(lo, jnp.broadcast_to(ms[h0] + jnp.log(ls[h0]), (BLK, LANES)),
                                               jnp.broadcast_to(ms[h1] + jnp.log(ls[h1]), (BLK, LANES)))
        return carry

    lax.fori_loop(0, dil * nb, block, 0, unroll=4)


def _dsw_prompt_attn(qkv_g, bias_full, g):
    batch, dil, sub, width = qkv_g.shape
    seq = dil * sub
    assert seq % WINDOWS[g] == 0 and dil == DILATIONS[g]
    nb = sub // BLK
    if nb == 1:
        bias_full = bias_full[:, :, BLK:]
    out = jax.ShapeDtypeStruct((batch, _PAIRS, seq, LANES), _F32)
    out_spec = pl.BlockSpec((None, _PAIRS, seq, LANES), lambda b: (b, 0, 0, 0))
    return pl.pallas_call(
        functools.partial(_dsw_prompt_kernel, dil=dil, nb=nb),
        grid=(batch,),
        in_specs=[pl.BlockSpec((None, dil, sub, width), lambda b: (b, 0, 0, 0)), _const_spec(bias_full.shape)],
        out_specs=[out_spec, out_spec],
        out_shape=[out, out],
        compiler_params=_params(1),
        name=f"dsw_prompt_attn_g{g}",
    )(qkv_g, bias_full)


def _merge_groups(os, ls):
    la, lb, lc = ls
    mx = jnp.maximum(jnp.maximum(la, lb), lc)
    ea, eb, ec = jnp.exp(la - mx), jnp.exp(lb - mx), jnp.exp(lc - mx)
    return (ea * os[0] + eb * os[1] + ec * os[2]) / (ea + eb + ec)


def _dsw_merge_slab_kernel(o0_ref, o1_ref, o2_ref, l0_ref, l1_ref, l2_ref, w_ref, x_ref, out_ref):
    o_refs, l_refs = (o0_ref, o1_ref, o2_ref), (l0_ref, l1_ref, l2_ref)
    slabs = [_merge_groups([r[p] for r in o_refs], [r[p] for r in l_refs]).astype(_BF) for p in range(_PAIRS)]
    o = jnp.concatenate(slabs, axis=1)
    out_ref[...] = x_ref[...] + jnp.dot(o, w_ref[...], preferred_element_type=_F32)


def _dsw_merge_slab(outs, lses, w_o, x3d, tm=512):
    batch, seq, d = x3d.shape
    spec = pl.BlockSpec((None, _PAIRS, tm, LANES), lambda b, s: (b, 0, s, 0))
    xspec = pl.BlockSpec((None, tm, d), lambda b, s: (b, s, 0))
    return pl.pallas_call(
        _dsw_merge_slab_kernel,
        grid=(batch, seq // tm),
        in_specs=[spec] * 6 + [_const_spec((D_ATT, d)), xspec],
        out_specs=xspec,
        out_shape=jax.ShapeDtypeStruct((batch, seq, d), _F32),
        compiler_params=_params(2),
        name="dsw_merge_out_prompt",
    )(*outs, *lses, w_o, x3d)


def _dsw_merge_kernel(o0_ref, o1_ref, o2_ref, l0_ref, l1_ref, l2_ref, w_ref, x_ref, out_ref):
    o = _merge_groups([o0_ref[...], o1_ref[...], o2_ref[...]], [l0_ref[...], l1_ref[...], l2_ref[...]])
    out_ref[...] = x_ref[...] + jnp.dot(o.astype(_BF), w_ref[...], preferred_element_type=_F32)


def _dsw_merge(outs, lses, w_o, x2d, tm=512):
    m, d = x2d.shape
    tm = min(tm, m)
    spec = pl.BlockSpec((tm, D_ATT), lambda i: (i, 0))
    return pl.pallas_call(
        _dsw_merge_kernel,
        grid=(m // tm,),
        in_specs=[spec] * 6 + [_const_spec((D_ATT, d)), pl.BlockSpec((tm, d), lambda i: (i, 0))],
        out_specs=pl.BlockSpec((tm, d), lambda i: (i, 0)),
        out_shape=jax.ShapeDtypeStruct((m, d), _F32),
        compiler_params=_params(1),
        name="dsw_merge_out",
    )(*outs, *lses, w_o, x2d)


_NT = (((1,), (1,)), ((), ()))


def _linear_t_kernel(x_ref, g_ref, wt_ref, o_ref, *, n_chunk):
    hb = _rms(x_ref[...], g_ref[...]).astype(_BF)
    for c0 in range(0, wt_ref.shape[0], n_chunk):
        o_ref[c0:c0 + n_chunk, :] = lax.dot_general(wt_ref[c0:c0 + n_chunk, :], hb, _NT,
                                                    preferred_element_type=_F32)


def _linear_t(x, norm_g, wt, name):
    m, k = x.shape
    n = wt.shape[0]
    return pl.pallas_call(
        functools.partial(_linear_t_kernel, n_chunk=512),
        grid=(1,),
        in_specs=[_const_spec((m, k)), _const_spec((1, k)), _const_spec((n, k))],
        out_specs=pl.BlockSpec((n, m), lambda i: (0, 0)),
        out_shape=jax.ShapeDtypeStruct((n, m), _F32),
        compiler_params=_params(1),
        name=name,
    )(x, norm_g.reshape(1, k), wt)


def _dsw_sample_kernel(qkv_ref, kvt_ref, cache_ref, bias_ref, b0_ref, o_ref, lse_ref, new_ref, *, g, window,
                       dil, rows):
    b = pl.program_id(0)
    base = g * 3 * D_ATT
    q = qkv_ref[:, base:base + D_ATT] * (DH_A ** -0.5)
    knew = qkv_ref[:, base + D_ATT:base + 2 * D_ATT]
    vnew = qkv_ref[:, base + 2 * D_ATT:base + 3 * D_ATT]
    head = lax.broadcasted_iota(jnp.int32, (H_A, D_ATT), 0)
    lane_head = lax.broadcasted_iota(jnp.int32, (H_A, D_ATT), 1) // DH_A
    hm = head == lane_head
    qrows = jnp.where(hm, jnp.broadcast_to(q, (H_A, D_ATT)), 0.0).astype(_BF)
    s = jnp.dot(qrows, cache_ref[0:D_ATT, :].astype(_BF), preferred_element_type=_F32)
    pos = lax.broadcasted_iota(jnp.int32, (H_A, window), 1)
    s = jnp.where((pos & (dil - 1)) == 0, s + bias_ref[...], NEG)
    prod = q.astype(_BF).astype(_F32) * knew.astype(_BF).astype(_F32)
    s_new = jnp.sum(jnp.where(hm, jnp.broadcast_to(prod, (H_A, D_ATT)), 0.0), axis=-1, keepdims=True) \
        + b0_ref[...]
    m = jnp.maximum(jnp.max(s, axis=-1, keepdims=True), s_new)
    pr = jnp.exp(s - m)
    p_new = jnp.exp(s_new - m)
    l = jnp.sum(pr, axis=-1, keepdims=True) + p_new
    o8 = lax.dot_general(pr.astype(_BF), cache_ref[D_ATT:2 * D_ATT, :].astype(_BF), _NT,
                         preferred_element_type=_F32)
    o8 = (o8 + p_new.astype(_BF).astype(_F32) * vnew.astype(_BF).astype(_F32)) / l
    lse8 = jnp.broadcast_to(m + jnp.log(l), (H_A, D_ATT))
    o_ref[...] = jnp.sum(jnp.where(hm, o8, 0.0), axis=0, keepdims=True)
    lse_ref[...] = jnp.sum(jnp.where(hm, lse8, 0.0), axis=0, keepdims=True)
    mine = lax.broadcasted_iota(jnp.int32, (rows, kvt_ref.shape[1]), 1) == b
    last = lax.broadcasted_iota(jnp.int32, (rows, LANES), 1) == LANES - 1
    for r0 in range(0, 2 * D_ATT, rows):
        rolled = pltpu.roll(cache_ref[r0:r0 + rows, :], window - 1, 1)
        col = jnp.sum(jnp.where(mine, kvt_ref[r0:r0 + rows, :], 0.0), axis=1, keepdims=True)
        if window > LANES:
            new_ref[r0:r0 + rows, :window - LANES] = rolled[:, :window - LANES]
        new_ref[r0:r0 + rows, window - LANES:] = jnp.where(last, col, rolled[:, window - LANES:])


def _dsw_sample(qkv_s, kvt_s, cache_t, bias_g, g):
    db = qkv_s.shape[0]
    window, dil = WINDOWS[g], DILATIONS[g]
    assert cache_t.shape[2] == window and dil & (dil - 1) == 0
    width = qkv_s.shape[-1]
    w = np.arange(window)
    back = np.where(w % dil == 0, (window - w) // dil, 0)
    bias_pos = bias_g[:, back]
    b0 = bias_g[:, 0:1]
    blk = (None, 2 * D_ATT, window)
    o, lse, new = pl.pallas_call(
        functools.partial(_dsw_sample_kernel, g=g, window=window, dil=dil, rows=128),
        grid=(db,),
        in_specs=[pl.BlockSpec((None, 1, width), lambda b: (b, 0, 0)),
                  pl.BlockSpec((None, 2 * D_ATT, db), lambda b: (g, 0, 0)),
                  pl.BlockSpec(blk, lambda b: (b, 0, 0)),
                  _const_spec((H_A, window)), _const_spec((H_A, 1))],
        out_specs=[pl.BlockSpec((None, 1, D_ATT), lambda b: (b, 0, 0)),
                   pl.BlockSpec((None, 1, D_ATT), lambda b: (b, 0, 0)),
                   pl.BlockSpec(blk, lambda b: (b, 0, 0))],
        out_shape=[jax.ShapeDtypeStruct((db, 1, D_ATT), _F32), jax.ShapeDtypeStruct((db, 1, D_ATT), _F32),
                   jax.ShapeDtypeStruct((db, 2 * D_ATT, window), _F32)],
        compiler_params=_params(1),
        name=f"dsw_sample_g{g}",
    )(qkv_s.reshape(db, 1, width), kvt_s, cache_t, bias_pos, b0)
    return o.reshape(db, D_ATT), lse.reshape(db, D_ATT), new


def _ffn_gate(ug, ug1, ug2, uv, uv1, uv2, cw_ref, cb_ref, cg, cv, chunk):
    def conv(u, u1, u2, c0):
        cs = slice(c0, c0 + chunk)
        return cb_ref[:, cs] + cw_ref[0:1, cs] * u2 + cw_ref[1:2, cs] * u1 + cw_ref[2:3, cs] * u
    yg = conv(ug, ug1, ug2, cg)
    yv = conv(uv, uv1, uv2, cv)
    return (yg * (1.0 / (1.0 + jnp.exp(-yg))) * yv).astype(_BF)


def _ffn_prompt_kernel(*refs, ts, d_ff, chunk, final_norm):
    if final_norm:
        x_ref, st_ref, g_ref, wup_ref, cw_ref, cb_ref, wdn_ref, gf_ref, o_ref, ns_ref, carry_ref, act_ref = refs
    else:
        x_ref, st_ref, g_ref, wup_ref, cw_ref, cb_ref, wdn_ref, o_ref, ns_ref, carry_ref, act_ref = refs
        gf_ref = None
    s = pl.program_id(1)

    @pl.when(s == 0)
    def _():
        carry_ref[...] = st_ref[...]

    x = x_ref[...]
    hb = _rms(x, g_ref[...]).astype(_BF)
    row = lax.broadcasted_iota(jnp.int32, (8, chunk), 0)
    first, second = row == 0, row == 1

    def up(c0):
        u = jnp.dot(hb, wup_ref[:, c0:c0 + chunk], preferred_element_type=_F32)
        p2 = carry_ref[0:1, c0:c0 + chunk]
        p1 = carry_ref[1:2, c0:c0 + chunk]
        r1, r2 = pltpu.roll(u, 1, 0), pltpu.roll(u, 2, 0)
        u1 = jnp.concatenate([jnp.where(first, p1, r1[:8]), r1[8:]], axis=0)
        u2 = jnp.concatenate([jnp.where(first, p2, jnp.where(second, p1, r2[:8])), r2[8:]], axis=0)
        carry_ref[:, c0:c0 + chunk] = u[ts - 2:ts, :]
        return u, u1, u2

    for c in range(d_ff // chunk):
        cg, cv = c * chunk, d_ff + c * chunk
        act_ref[:, cg:cg + chunk] = _ffn_gate(*up(cg), *up(cv), cw_ref, cb_ref, cg, cv, chunk)
    y = x + jnp.dot(act_ref[...], wdn_ref[...], preferred_element_type=_F32)
    if final_norm:
        y = _rms(y, gf_ref[...])
    o_ref[...] = y

    @pl.when(s == pl.num_programs(1) - 1)
    def _():
        ns_ref[...] = carry_ref[...]


def _ffn_prompt(x3d, state, g, w_up, conv_w, conv_b, w_down, g_final=None, ts=512, chunk=256):
    batch, seq, d = x3d.shape
    d_ff = w_down.shape[0]
    assert d_ff % chunk == 0 and seq % ts == 0
    final_norm = g_final is not None
    args = [x3d, state, g.reshape(1, d), w_up, conv_w, conv_b.reshape(1, 2 * d_ff), w_down]
    specs = [pl.BlockSpec((None, ts, d), lambda b, s: (b, s, 0)),
             pl.BlockSpec((None, CONV_W - 1, 2 * d_ff), lambda b, s: (b, 0, 0)),
             _const_spec((1, d)), _const_spec((d, 2 * d_ff)), _const_spec((CONV_W, 2 * d_ff)),
             _const_spec((1, 2 * d_ff)), _const_spec((d_ff, d))]
    if final_norm:
        args.append(g_final.reshape(1, d))
        specs.append(_const_spec((1, d)))
    return pl.pallas_call(
        functools.partial(_ffn_prompt_kernel, ts=ts, d_ff=d_ff, chunk=chunk, final_norm=final_norm),
        grid=(batch, seq // ts),
        in_specs=specs,
        out_specs=[pl.BlockSpec((None, ts, d), lambda b, s: (b, s, 0)),
                   pl.BlockSpec((None, CONV_W - 1, 2 * d_ff), lambda b, s: (b, 0, 0))],
        out_shape=[jax.ShapeDtypeStruct((batch, seq, d), _F32),
                   jax.ShapeDtypeStruct((batch, CONV_W - 1, 2 * d_ff), _F32)],
        scratch_shapes=[pltpu.VMEM((CONV_W - 1, 2 * d_ff), _F32), pltpu.VMEM((ts, d_ff), _BF)],
        compiler_params=_params(2),
        name="conv_ffn_prompt",
    )(*args)


def _ffn_sample_kernel(*refs, d_ff, chunk, final_norm):
    if final_norm:
        x_ref, p2_ref, p1_ref, g_ref, wup_ref, cw_ref, cb_ref, wdn_ref, gf_ref, o_ref, u_ref, act_ref = refs
    else:
        x_ref, p2_ref, p1_ref, g_ref, wup_ref, cw_ref, cb_ref, wdn_ref, o_ref, u_ref, act_ref = refs
        gf_ref = None
    x = x_ref[...]
    hb = _rms(x, g_ref[...]).astype(_BF)

    def up(c0):
        cs = slice(c0, c0 + chunk)
        u = jnp.dot(hb, wup_ref[:, cs], preferred_element_type=_F32)
        u_ref[:, cs] = u
        return u, p1_ref[:, cs], p2_ref[:, cs]

    for c in range(d_ff // chunk):
        cg, cv = c * chunk, d_ff + c * chunk
        act_ref[:, cg:cg + chunk] = _ffn_gate(*up(cg), *up(cv), cw_ref, cb_ref, cg, cv, chunk)
    y = x + jnp.dot(act_ref[...], wdn_ref[...], preferred_element_type=_F32)
    if final_norm:
        y = _rms(y, gf_ref[...])
    o_ref[...] = y


def _ffn_sample(x2d, prev2, prev1, g, w_up, conv_w, conv_b, w_down, g_final=None, chunk=256):
    db, d = x2d.shape
    d_ff = w_down.shape[0]
    final_norm = g_final is not None
    args = [x2d, prev2, prev1, g.reshape(1, d), w_up, conv_w, conv_b.reshape(1, 2 * d_ff), w_down]
    specs = [_const_spec((db, d)), _const_spec((db, 2 * d_ff)), _const_spec((db, 2 * d_ff)),
             _const_spec((1, d)), _const_spec((d, 2 * d_ff)), _const_spec((CONV_W, 2 * d_ff)),
             _const_spec((1, 2 * d_ff)), _const_spec((d_ff, d))]
    if final_norm:
        args.append(g_final.reshape(1, d))
        specs.append(_const_spec((1, d)))
    return pl.pallas_call(
        functools.partial(_ffn_sample_kernel, d_ff=d_ff, chunk=chunk, final_norm=final_norm),
        grid=(1,),
        in_specs=specs,
        out_specs=[pl.BlockSpec((db, d), lambda i: (0, 0)), pl.BlockSpec((db, 2 * d_ff), lambda i: (0, 0))],
        out_shape=[jax.ShapeDtypeStruct((db, d), _F32), jax.ShapeDtypeStruct((db, 2 * d_ff), _F32)],
        scratch_shapes=[pltpu.VMEM((db, d_ff), _BF)],
        compiler_params=_params(1),
        name="conv_ffn_sample",
    )(*args)


def _mla_proj_kernel(x_ref, ck_ref, sk_ref, g_ref, win_ref, gq_ref, gkv_ref, wq_ref, wk_ref, wv_ref,
                     q_ref, k_ref, v_ref, ckv_ref, kr_ref):
    hb = _rms(x_ref[...], g_ref[...]).astype(_BF)
    proj = jnp.dot(hb, win_ref[...], preferred_element_type=_F32)
    c_q = _rms(proj[:, :Q_LORA], gq_ref[...])
    c_kv = _rms(proj[:, Q_LORA:Q_LORA + KV_LORA], gkv_ref[...])
    ckv_ref[...] = c_kv
    off = Q_LORA + KV_LORA
    ck, sk = ck_ref[...], sk_ref[...]
    kr_pad = proj[:, off:off + LANES] * ck + proj[:, off + LANES:off + 2 * LANES] * sk
    kr_ref[...] = kr_pad[:, QK_NOPE:QK_NOPE + QK_ROPE]
    scale = (QK_NOPE + QK_ROPE) ** -0.5 * LOG2E
    lane = lax.broadcasted_iota(jnp.int32, ck.shape, 1)
    nope = (lane < QK_NOPE).astype(_F32)
    ones_col = (lane == V_HEAD).astype(_F32)
    cq_tab = scale * (ck + nope)
    sq_tab = scale * sk
    cqb = c_q.astype(_BF)
    ckvb = c_kv.astype(_BF)
    hw = H_B * LANES
    for hp in range(H_B // 2):
        sl = slice(2 * hp * LANES, 2 * (hp + 1) * LANES)
        qa = jnp.dot(cqb, wq_ref[:, sl], preferred_element_type=_F32)
        qs = jnp.dot(cqb, wq_ref[:, hw + 2 * hp * LANES:hw + 2 * (hp + 1) * LANES], preferred_element_type=_F32)
        kn = jnp.dot(ckvb, wk_ref[:, sl], preferred_element_type=_F32)
        vv = jnp.dot(ckvb, wv_ref[:, sl], preferred_element_type=_F32)
        for hh in range(2):
            h, one = 2 * hp + hh, slice(hh * LANES, (hh + 1) * LANES)
            q_ref[h] = (qa[:, one] * cq_tab + qs[:, one] * sq_tab).astype(_BF)
            k_ref[h] = (kn[:, one] + kr_pad).astype(_BF)
            v_ref[h] = (vv[:, one] + ones_col).astype(_BF)


def _mla_proj(x2d, ck_tab, sk_tab, g, w_in_ext, g_q, g_kv, w_q_both, w_k_pad, w_v, n_seq, tm):
    m, d = x2d.shape
    rows = m // n_seq
    tm = min(tm, rows)
    tps = rows // tm
    x_map = lambda s, b: (b * tps + s, 0)
    t_map = lambda s, b: (s, 0)
    return pl.pallas_call(
        _mla_proj_kernel,
        grid=(tps, n_seq),
        in_specs=[pl.BlockSpec((tm, d), x_map), pl.BlockSpec((tm, LANES), t_map), pl.BlockSpec((tm, LANES), t_map),
                  _const_spec((1, d)), _const_spec(w_in_ext.shape), _const_spec((1, Q_LORA)),
                  _const_spec((1, KV_LORA)), _const_spec(w_q_both.shape), _const_spec(w_k_pad.shape),
                  _const_spec(w_v.shape)],
        out_specs=[pl.BlockSpec((H_B, tm, LANES), lambda s, b: (0, b * tps + s, 0)),
                   pl.BlockSpec((H_B, tm, LANES), lambda s, b: (0, b * tps + s, 0)),
                   pl.BlockSpec((H_B, tm, LANES), lambda s, b: (0, b * tps + s, 0)),
                   pl.BlockSpec((tm, KV_LORA), x_map), pl.BlockSpec((tm, QK_ROPE), x_map)],
        out_shape=[jax.ShapeDtypeStruct((H_B, m, LANES), _BF), jax.ShapeDtypeStruct((H_B, m, LANES), _BF),
                   jax.ShapeDtypeStruct((H_B, m, LANES), _BF),
                   jax.ShapeDtypeStruct((m, KV_LORA), _F32), jax.ShapeDtypeStruct((m, QK_ROPE), _F32)],
        compiler_params=_params(2),
        name="mla_proj",
    )(x2d, ck_tab, sk_tab, g.reshape(1, d), w_in_ext, g_q.reshape(1, Q_LORA), g_kv.reshape(1, KV_LORA),
      w_q_both, w_k_pad, w_v)


def _mla_prompt_kernel(q_ref, k_ref, v_ref, o_ref, s_ref, *, seq, tq, ck):
    lo = lax.broadcasted_iota(jnp.int32, (tq, LANES), 1) < V_HEAD
    row = lax.broadcasted_iota(jnp.int32, (tq, tq), 0)
    colk = lax.broadcasted_iota(jnp.int32, (tq, tq), 1)
    causal = colk <= row
    units = [(qi, hh) for qi in range(seq // tq) for hh in range(2)]
    state = [dict(mx=None, m=None, acc=None) for _ in units]
    outs = {}

    def chunks_of(qi):
        kv_len = (qi + 1) * tq
        return [(c0, min(ck, kv_len - c0)) for c0 in range(0, kv_len, ck)]

    def score_tasks(u):
        (qi, hh), st, slot = units[u], state[u], u % 2
        q0, kv_len = qi * tq, (qi + 1) * tq

        def task(c0, cw):
            s = lax.dot_general(q_ref[hh, q0:q0 + tq, :], k_ref[hh, c0:c0 + cw, :], _NT,
                                preferred_element_type=_F32)
            if c0 + cw == kv_len:
                sd = jnp.where(causal, s[:, cw - tq:], NEG)
                if cw > tq:
                    s_ref[slot, :, c0:c0 + cw - tq] = s[:, :cw - tq]
                s_ref[slot, :, kv_len - tq:kv_len] = sd
                tiles = [s[:, j:j + LANES] for j in range(0, cw - tq, LANES)]
                tiles += [sd[:, j:j + LANES] for j in range(0, tq, LANES)]
            else:
                s_ref[slot, :, c0:c0 + cw] = s
                tiles = [s[:, j:j + LANES] for j in range(0, cw, LANES)]
            for t in tiles:
                st["mx"] = t if st["mx"] is None else jnp.maximum(st["mx"], t)

        return [functools.partial(task, c0, cw) for c0, cw in chunks_of(qi)]

    def value_tasks(u):
        (qi, hh), st, slot = units[u], state[u], u % 2
        q0 = qi * tq

        def task(c0, cw, first, last):
            if first:
                st["m"] = jnp.max(st["mx"], axis=-1, keepdims=True)
            p = jnp.exp2(s_ref[slot, :, c0:c0 + cw] - st["m"]).astype(_BF)
            pv = jnp.dot(p, v_ref[hh, c0:c0 + cw, :], preferred_element_type=_F32)
            st["acc"] = pv if st["acc"] is None else st["acc"] + pv
            if last:
                outs[hh] = st["acc"] / st["acc"][:, V_HEAD:V_HEAD + 1]
                if hh == 1:
                    o_ref[q0:q0 + tq, :] = jnp.where(lo, outs[0], pltpu.roll(outs[1], V_HEAD, 1)).astype(o_ref.dtype)

        ch = chunks_of(qi)
        return [functools.partial(task, c0, cw, i == 0, i == len(ch) - 1) for i, (c0, cw) in enumerate(ch)]

    pending = []
    for u in range(len(units) + 1):
        scoring = score_tasks(u) if u < len(units) else []
        for i in range(max(len(scoring), len(pending))):
            if i < len(scoring):
                scoring[i]()
            if i < len(pending):
                pending[i]()
        pending = value_tasks(u) if u < len(units) else []


def _mla_prompt_attn(q, k, v, batch, seq, tq=256, ck=512):
    m = batch * seq
    blk = pl.BlockSpec((2, seq, LANES), lambda b, hp: (hp, b, 0))
    return pl.pallas_call(
        functools.partial(_mla_prompt_kernel, seq=seq, tq=tq, ck=ck),
        grid=(batch, H_B // 2),
        in_specs=[blk, blk, blk],
        out_specs=pl.BlockSpec((seq, LANES), lambda b, hp: (b, hp)),
        out_shape=jax.ShapeDtypeStruct((m, H_B * V_HEAD), _BF),
        scratch_shapes=[pltpu.VMEM((2, tq, seq), _F32)],
        compiler_params=_params(2),
        name="mla_prompt_attn",
    )(q, k, v)


def _mla_absorb_kernel(q_ref, wuk_ref, o_ref):
    for h in range(H_B):
        o_ref[h] = jnp.dot(q_ref[h], wuk_ref[h], preferred_element_type=_F32).astype(o_ref.dtype)


def _mla_absorb(q, w_uk_t_pad):
    db = q.shape[1]
    return pl.pallas_call(
        _mla_absorb_kernel,
        grid=(1,),
        in_specs=[_const_spec(q.shape), _const_spec(w_uk_t_pad.shape)],
        out_specs=pl.BlockSpec((H_B, db, KV_LORA), lambda i: (0, 0, 0)),
        out_shape=jax.ShapeDtypeStruct((H_B, db, KV_LORA), _BF),
        compiler_params=_params(1),
        name="mla_absorb_q",
    )(q, w_uk_t_pad)


def _mla_sample_kernel(pt_ref, qlat_ref, qrope_ref, cnew_ref, krnew_ref, ckv_hbm, kr_hbm, o_ref,
                       ckv_buf, kr_buf, sems, *, pages, n_pages):
    b = pl.program_id(0)
    n_chunks = n_pages // pages

    def copies(page_of, slot):
        out = []
        for t in range(pages):
            page = page_of(t)
            out.append(pltpu.make_async_copy(ckv_hbm.at[page], ckv_buf.at[slot, t], sems.at[0, slot]))
            out.append(pltpu.make_async_copy(kr_hbm.at[page], kr_buf.at[slot, t], sems.at[1, slot]))
        return out

    def start(seq, chunk, slot):
        for cp in copies(lambda t: pt_ref[seq * n_pages + chunk * pages + t], slot):
            cp.start()

    def wait(slot):
        for cp in copies(lambda t: 0, slot):
            cp.wait()

    @pl.when(b == 0)
    def _():
        start(0, 0, 0)

    qlat = qlat_ref[...]
    qrope = qrope_ref[...]

    def update(carry, s, weighted_values):
        m_prev, l_prev, acc = carry
        m_new = jnp.maximum(m_prev, jnp.max(s, axis=-1, keepdims=True))
        alpha = jnp.exp2(m_prev - m_new)
        p = jnp.exp2(s - m_new)
        return (m_new, alpha * l_prev + jnp.sum(p, axis=-1, keepdims=True),
                alpha * acc + weighted_values(p.astype(_BF)))

    carry = (jnp.full((H_B, 1), -jnp.inf, _F32), jnp.zeros((H_B, 1), _F32), jnp.zeros((H_B, KV_LORA), _F32))
    for c in range(n_chunks):
        slot = c % 2
        if c + 1 < n_chunks:
            start(b, c + 1, 1 - slot)
        else:
            @pl.when(b + 1 < pl.num_programs(0))
            def _():
                start(b + 1, 0, 1 - slot)
        wait(slot)
        ckv = ckv_buf[slot].reshape(pages * ckv_buf.shape[2], KV_LORA).astype(_BF)
        s_rope = jnp.concatenate(
            [jnp.dot(qrope, kr_buf[slot, t].astype(_BF), preferred_element_type=_F32) for t in range(pages)],
            axis=1)
        s = lax.dot_general(qlat, ckv, _NT, preferred_element_type=_F32) + s_rope
        carry = update(carry, s, lambda pb, ckv=ckv: jnp.dot(pb, ckv, preferred_element_type=_F32))

    cn = cnew_ref[...].astype(_BF).astype(_F32)
    krn = krnew_ref[...].astype(_BF).astype(_F32)
    s = jnp.sum(qlat.astype(_F32) * cn, axis=-1, keepdims=True) \
        + jnp.sum(qrope.astype(_F32) * krn, axis=-1, keepdims=True)
    _, l, acc = update(carry, s, lambda pb: pb.astype(_F32) * cn)
    o_ref[...] = (acc / l).astype(o_ref.dtype)


def _mla_sample_attn(page_table, qlat, qrope, c_new, kr_new, ckv_pool, kr_pool_t, pages=32):
    db, n_pages = page_table.shape
    page = ckv_pool.shape[1]
    pages = min(pages, n_pages // 2)
    assert n_pages % (2 * pages) == 0
    per_b = lambda b, pt: (b, 0, 0)
    grid_spec = pltpu.PrefetchScalarGridSpec(
        num_scalar_prefetch=1,
        grid=(db,),
        in_specs=[pl.BlockSpec((None, H_B, KV_LORA), per_b), pl.BlockSpec((None, H_B, QK_ROPE), per_b),
                  pl.BlockSpec((None, 1, KV_LORA), per_b), pl.BlockSpec((None, 1, QK_ROPE), per_b),
                  pl.BlockSpec(memory_space=pl.ANY), pl.BlockSpec(memory_space=pl.ANY)],
        out_specs=pl.BlockSpec((None, H_B, KV_LORA), per_b),
        scratch_shapes=[pltpu.VMEM((2, pages, page, KV_LORA), _F32), pltpu.VMEM((2, pages, QK_ROPE, page), _F32),
                        pltpu.SemaphoreType.DMA((2, 2))],
    )
    return pl.pallas_call(
        functools.partial(_mla_sample_kernel, pages=pages, n_pages=n_pages),
        grid_spec=grid_spec,
        out_shape=jax.ShapeDtypeStruct((db, H_B, KV_LORA), _BF),
        compiler_params=_params(1),
        name="mla_sample_attn",
    )(page_table.reshape(-1), qlat, qrope, c_new, kr_new, ckv_pool, kr_pool_t)


def _mla_unabsorb_kernel(o_ref, wuv_ref, out_ref):
    for p in range(H_B // 2):
        y = jnp.dot(o_ref[2 * p], wuv_ref[2 * p], preferred_element_type=_F32) \
            + jnp.dot(o_ref[2 * p + 1], wuv_ref[2 * p + 1], preferred_element_type=_F32)
        out_ref[:, p * LANES:(p + 1) * LANES] = y.astype(out_ref.dtype)


def _mla_unabsorb(o_lat_hm, w_uv_pair):
    db = o_lat_hm.shape[1]
    return pl.pallas_call(
        _mla_unabsorb_kernel,
        grid=(1,),
        in_specs=[_const_spec(o_lat_hm.shape), _const_spec(w_uv_pair.shape)],
        out_specs=pl.BlockSpec((db, H_B * V_HEAD), lambda i: (0, 0)),
        out_shape=jax.ShapeDtypeStruct((db, H_B * V_HEAD), _BF),
        compiler_params=_params(1),
        name="mla_unabsorb_o",
    )(o_lat_hm, w_uv_pair)


def _t5_bucket(dist):
    n = np.asarray(dist)
    max_exact = N_BUCKETS // 2
    large = max_exact + (np.log(np.maximum(n, 1) / max_exact) / np.log(T5_MAX_DISTANCE / max_exact)
                         * (N_BUCKETS - max_exact)).astype(np.int32)
    large = np.minimum(large, N_BUCKETS - 1)
    return np.where(n < max_exact, n, large).astype(np.int32)


def _dsw_biases(table):
    out = []
    for g, (w, d) in enumerate(zip(WINDOWS, DILATIONS)):
        buckets = _t5_bucket(np.arange(w // d + 1) * d)
        out.append(table[buckets][:, g * H_A:(g + 1) * H_A].T.astype(_F32))
    return out


def _dsw_bias_blocks(bias_g):
    n = 3 * BLK
    diag = bias_g[:, np.clip(2 * BLK - 1 - np.arange(n), 0, BLK)]
    skew = jnp.tile(diag, (1, BLK))[:, :BLK * (n - 1)].reshape(bias_g.shape[0], BLK, n - 1)
    return skew[:, :, BLK - 1:3 * BLK - 1]


def _rope_pad_tables(pos):
    inv = jnp.asarray(ROPE_BASE ** (-np.arange(0, QK_ROPE, 2) / QK_ROPE), dtype=_F32)
    ang = pos.astype(_F32)[:, None] * inv[None, :]
    cos, sin = jnp.cos(ang), jnp.sin(ang)
    n = pos.shape[0]
    zl = jnp.zeros((n, QK_NOPE), _F32)
    zr = jnp.zeros((n, LANES - QK_NOPE - QK_ROPE), _F32)
    return (jnp.concatenate([zl, cos, cos, zr], axis=1), jnp.concatenate([zl, -sin, sin, zr], axis=1))


def _mla_weights(w_in, w_q, w_kv):
    d = w_in.shape[0]
    half = QK_ROPE // 2
    off = Q_LORA + KV_LORA
    kr = w_in[:, off:]
    kr_sw = jnp.concatenate([kr[:, half:], kr[:, :half]], axis=1)
    zl = jnp.zeros((d, QK_NOPE), w_in.dtype)
    zr = jnp.zeros((d, LANES - QK_NOPE - QK_ROPE), w_in.dtype)
    w_in_ext = jnp.concatenate([w_in[:, :off], zl, kr, zr, zl, kr_sw, zr], axis=1).astype(_BF)

    dh = QK_NOPE + QK_ROPE
    wq3 = w_q.reshape(Q_LORA, H_B, dh)
    pad = jnp.zeros((Q_LORA, H_B, LANES - dh), w_q.dtype)
    wq_pad = jnp.concatenate([wq3, pad], axis=2)
    wq_sw = jnp.concatenate([jnp.zeros((Q_LORA, H_B, QK_NOPE), w_q.dtype), wq3[:, :, QK_NOPE + half:],
                             wq3[:, :, QK_NOPE:QK_NOPE + half], pad], axis=2)
    w_q_both = jnp.concatenate([wq_pad.reshape(Q_LORA, H_B * LANES), wq_sw.reshape(Q_LORA, H_B * LANES)],
                               axis=1).astype(_BF)

    wkv3 = w_kv.reshape(KV_LORA, H_B, QK_NOPE + V_HEAD)
    w_uk, w_uv = wkv3[:, :, :QK_NOPE], wkv3[:, :, QK_NOPE:]
    w_k_pad = jnp.concatenate([w_uk, jnp.zeros((KV_LORA, H_B, LANES - QK_NOPE), w_kv.dtype)], axis=2)
    w_k_pad = w_k_pad.reshape(KV_LORA, H_B * LANES).astype(_BF)
    w_v = jnp.concatenate([w_uv, jnp.zeros((KV_LORA, H_B, LANES - V_HEAD), w_kv.dtype)], axis=2)
    w_v = w_v.reshape(KV_LORA, H_B * LANES).astype(_BF)
    w_uk_t = jnp.transpose(w_uk, (1, 2, 0))
    w_uk_t_pad = jnp.concatenate([w_uk_t, jnp.zeros((H_B, LANES - QK_NOPE, KV_LORA), w_kv.dtype)], axis=1)
    w_uv_h = jnp.transpose(w_uv, (1, 0, 2))
    zero = jnp.zeros_like(w_uv_h)
    even = jnp.concatenate([w_uv_h, zero], axis=2)
    odd = jnp.concatenate([zero, w_uv_h], axis=2)
    is_even = (jnp.arange(H_B) % 2 == 0)[:, None, None]
    w_uv_pair = jnp.where(is_even, even, odd)
    return w_in_ext, w_q_both, w_k_pad, w_v, w_uk_t_pad.astype(_BF), w_uv_pair.astype(_BF)


def kernel(x_prompt, x_sample, cache_dsw_g0, cache_dsw_g1, cache_dsw_g2, cache_mla_ckv, cache_mla_kr,
           state_ffn_conv, page_table, rel_bias_table, norm_mix, norm_ffn, norm_final, w_qkv_dsw, w_o_dsw,
           w_in_mla, g_q_mla, g_kv_mla, w_q_mla, w_kv_mla, w_o_mla, w_up_ffn, conv_w_ffn, conv_b_ffn, w_down_ffn):
    batch, seq, d = x_prompt.shape
    db, t_new, _ = x_sample.shape
    assert t_new == 1
    depth = norm_mix.shape[0]
    d_ff = w_down_ffn.shape[1]
    past_len = page_table.shape[1] * cache_mla_ckv.shape[2]
    m = batch * seq
    dsw_caches = (cache_dsw_g0, cache_dsw_g1, cache_dsw_g2)
    biases = _dsw_biases(rel_bias_table)
    bias_blocks = [_dsw_bias_blocks(bg) for bg in biases]

    xp = x_prompt.reshape(m, d)
    xs = x_sample.reshape(db, d)
    dsw_new_p = [[] for _ in range(N_GROUPS)]
    dsw_new_s = [[] for _ in range(N_GROUPS)]
    ckv_p, ckv_s, kr_p, kr_s, conv_p, conv_s = [], [], [], [], [], []
    zero_state = jnp.zeros((batch, CONV_W - 1, 2 * d_ff), _F32)

    for i in range(depth):
        if i % 2 == 0:
            a = i // 2
            w_qkv = w_qkv_dsw[a].astype(_BF)
            w_o = w_o_dsw[a].astype(_BF)
            *qkv_groups, c0, c1, c2 = _qkv_prompt(xp.reshape(batch, seq, d), norm_mix[i], w_qkv)
            for g, c in enumerate((c0, c1, c2)):
                c = c.reshape(batch, 2, H_A, DH_A, min(WINDOWS[g], seq))
                dsw_new_p[g].append(jnp.transpose(c, (0, 4, 1, 2, 3)))
            outs, lses = zip(*[_dsw_prompt_attn(qkv_groups[g], bias_blocks[g], g) for g in range(N_GROUPS)])
            xp = _dsw_merge_slab(outs, lses, w_o, xp.reshape(batch, seq, d)).reshape(m, d)
            qkv_s = _linear(xs, w_qkv, norm_g=norm_mix[i], name="dsw_qkv_sample")
            w_kv_t = jnp.transpose(w_qkv.reshape(d, N_GROUPS, 3, D_ATT)[:, :, 1:], (1, 2, 3, 0))
            kvt_s = _linear_t(xs, norm_mix[i], w_kv_t.reshape(N_GROUPS * 2 * D_ATT, d), "dsw_kv_sample_t")
            kvt_s = kvt_s.reshape(N_GROUPS, 2 * D_ATT, db)
            outs, lses = [], []
            for g in range(N_GROUPS):
                cache_t = jnp.transpose(dsw_caches[g][a], (0, 2, 3, 4, 1)).reshape(db, 2 * D_ATT, WINDOWS[g])
                o, lse, new = _dsw_sample(qkv_s, kvt_s, cache_t, biases[g], g)
                outs.append(o)
                lses.append(lse)
                new = new.reshape(db, 2, H_A, DH_A, WINDOWS[g])
                dsw_new_s[g].append(jnp.transpose(new, (0, 4, 1, 2, 3)))
            xs = _dsw_merge(outs, lses, w_o, xs)
        else:
            b = i // 2
            w_in_ext, w_q_both, w_k_pad, w_v, w_uk_t_pad, w_uv_pair = _mla_weights(
                w_in_mla[b], w_q_mla[b], w_kv_mla[b])
            w_o = w_o_mla[b].astype(_BF)
            ck_tab, sk_tab = _rope_pad_tables(jnp.arange(seq))
            q, k, v, ckv, kr = _mla_proj(xp, ck_tab, sk_tab, norm_mix[i], w_in_ext, g_q_mla[b], g_kv_mla[b],
                                         w_q_both, w_k_pad, w_v, n_seq=batch, tm=512)
            ckv_p.append(ckv.reshape(batch, seq, KV_LORA))
            kr_p.append(kr.reshape(batch, seq, QK_ROPE))
            o = _mla_prompt_attn(q, k, v, batch, seq)
            xp = _linear(o, w_o, residual=xp, name="mla_out_prompt")
            ck_s, sk_s = _rope_pad_tables(jnp.full((db,), past_len, jnp.int32))
            q, _, _, ckv, kr = _mla_proj(xs, ck_s, sk_s, norm_mix[i], w_in_ext, g_q_mla[b], g_kv_mla[b],
                                         w_q_both, w_k_pad, w_v, n_seq=1, tm=db)
            ckv_s.append(ckv.reshape(db, 1, KV_LORA))
            kr_s.append(kr.reshape(db, 1, QK_ROPE))
            qlat = jnp.transpose(_mla_absorb(q, w_uk_t_pad), (1, 0, 2))
            qrope = jnp.transpose(q[:, :, QK_NOPE:QK_NOPE + QK_ROPE], (1, 0, 2))
            o_lat = _mla_sample_attn(page_table, qlat, qrope, ckv.reshape(db, 1, KV_LORA),
                                     kr.reshape(db, 1, QK_ROPE), cache_mla_ckv[b],
                                     jnp.transpose(cache_mla_kr[b], (0, 2, 1)))
            o = _mla_unabsorb(jnp.transpose(o_lat, (1, 0, 2)), w_uv_pair)
            xs = _linear(o, w_o, residual=xs, name="mla_out_sample")

        g_final = norm_final if i == depth - 1 else None
        w_up = w_up_ffn[i].astype(_BF)
        w_dn = w_down_ffn[i].astype(_BF)
        yp, cp = _ffn_prompt(xp.reshape(batch, seq, d), zero_state, norm_ffn[i], w_up, conv_w_ffn[i],
                             conv_b_ffn[i], w_dn, g_final)
        xp = yp.reshape(m, d)
        conv_p.append(cp)
        prev2, prev1 = state_ffn_conv[i, :, 0], state_ffn_conv[i, :, 1]
        xs, u_s = _ffn_sample(xs, prev2, prev1, norm_ffn[i], w_up, conv_w_ffn[i], conv_b_ffn[i], w_dn, g_final)
        conv_s.append(jnp.stack([prev1, u_s], axis=1))

    def stack(per_layer):
        return per_layer[0][None] if len(per_layer) == 1 else jnp.stack(per_layer)

    y_prompt = xp.reshape(batch, seq, d)
    y_sample = xs.reshape(db, 1, d)
    return (y_prompt, y_sample,
            stack(dsw_new_p[0]), stack(dsw_new_s[0]), stack(dsw_new_p[1]), stack(dsw_new_s[1]),
            stack(dsw_new_p[2]), stack(dsw_new_s[2]),
            stack(ckv_p), stack(ckv_s), stack(kr_p), stack(kr_s), stack(conv_p), stack(conv_s))
```

```python
import functools

import numpy as np
import jax
import jax.numpy as jnp
from jax import lax
from jax.experimental import pallas as pl
from jax.experimental.pallas import tpu as pltpu

WINDOWS = (128, 512, 2048)
DILATIONS = (1, 4, 16)
N_GROUPS = 3
H_A = 8
DH_A = 64
D_ATT = H_A * DH_A
BLK = 128
N_BUCKETS = 32
T5_MAX_DISTANCE = 2048
H_B = 16
Q_LORA = 384
KV_LORA = 256
QK_NOPE = 64
QK_ROPE = 32
V_HEAD = 64
ROPE_BASE = 10000.0
CONV_W = 3
EPS = 1e-6
NEG = -1e30
LOG2E = 1.4426950408889634

LANES = 128
VMEM_LIMIT = 56 * 1024 * 1024

_BF = jnp.bfloat16
_F32 = jnp.float32


def _params(n_axes, vmem=VMEM_LIMIT):
    return pltpu.CompilerParams(dimension_semantics=("arbitrary",) * n_axes, vmem_limit_bytes=vmem)


def _const_spec(shape):
    nd = len(shape)
    return pl.BlockSpec(shape, lambda *_: (0,) * nd, pipeline_mode=pl.Buffered(1))


def _rms(x, g):
    return x * lax.rsqrt(jnp.mean(x * x, axis=-1, keepdims=True) + EPS) * g


def _linear_kernel(*refs, has_norm, has_res, n_chunk):
    it = iter(refs)
    x_ref = next(it)
    g_ref = next(it) if has_norm else None
    w_ref = next(it)
    r_ref = next(it) if has_res else None
    o_ref = next(it)
    x = x_ref[...]
    if has_norm:
        x = _rms(x.astype(_F32), g_ref[...])
    xb = x.astype(_BF)
    n = w_ref.shape[1]
    for c0 in range(0, n, n_chunk):
        y = jnp.dot(xb, w_ref[:, c0:c0 + n_chunk], preferred_element_type=_F32)
        if has_res:
            y = y + r_ref[:, c0:c0 + n_chunk]
        o_ref[:, c0:c0 + n_chunk] = y.astype(o_ref.dtype)


def _linear(x, w, *, norm_g=None, residual=None, out_dtype=_F32, tm=512, name="linear"):
    m, k = x.shape
    n = w.shape[1]
    tm = min(tm, m)
    assert m % tm == 0
    n_chunk = 512 if n % 512 == 0 else n
    args, specs = [x], [pl.BlockSpec((tm, k), lambda i: (i, 0))]
    if norm_g is not None:
        args.append(norm_g.reshape(1, k))
        specs.append(_const_spec((1, k)))
    args.append(w)
    specs.append(_const_spec((k, n)))
    if residual is not None:
        args.append(residual)
        specs.append(pl.BlockSpec((tm, n), lambda i: (i, 0)))
    return pl.pallas_call(
        functools.partial(_linear_kernel, has_norm=norm_g is not None, has_res=residual is not None,
                          n_chunk=n_chunk),
        grid=(m // tm,),
        in_specs=specs,
        out_specs=pl.BlockSpec((tm, n), lambda i: (i, 0)),
        out_shape=jax.ShapeDtypeStruct((m, n), out_dtype),
        compiler_params=_params(1),
        name=name,
    )(*args)


_PAIRS = H_A // 2
_QKV_SLOTS = 3


def _qkv_prompt_kernel(x_ref, g_ref, w_ref, q0_ref, q1_ref, q2_ref, c0_ref, c1_ref, c2_ref, scr_ref, hb_ref,
                       scr2_ref, *, tm, tiles_per_seq, keeps):
    last_tile = pl.program_id(1) == tiles_per_seq - 1
    hb_ref[...] = _rms(x_ref[...], g_ref[...]).astype(_BF)
    out_refs = (q0_ref, q1_ref, q2_ref)
    cache_refs = (c0_ref, c1_ref, c2_ref)

    def project(k):
        for half in range(2):
            c0 = k * D_ATT + half * 2 * LANES
            y = jnp.dot(hb_ref[...], w_ref[:, c0:c0 + 2 * LANES], preferred_element_type=_F32)
            if k % 3 == 0:
                y = y * (DH_A ** -0.5)
            for p in range(2):
                scr_ref[k % _QKV_SLOTS, 2 * half + p] = y[:, p * LANES:(p + 1) * LANES]

    def emit(k):
        g, part = divmod(k, 3)
        dil = DILATIONS[g]
        rows = tm // dil
        oref, cref, keep = out_refs[g], cache_refs[g], keeps[g]
        staged, fine = scr_ref.at[k % _QKV_SLOTS], dil
        if dil > 4:
            fine, grp = dil // 4, tm // 4
            for r4 in range(4):
                for p in range(_PAIRS):
                    scr2_ref[p, r4 * grp:(r4 + 1) * grp, :] = staged[p, pl.ds(r4, grp, stride=4), :]
            staged = scr2_ref
        for r in range(dil):
            for p in range(_PAIRS):
                c0 = part * D_ATT + p * LANES
                if dil == 1:
                    src = pl.ds(0, tm)
                elif dil > 4:
                    src = pl.ds((r % 4) * grp + r // 4, rows, stride=fine)
                else:
                    src = pl.ds(r, rows, stride=dil)
                oref[r, :, c0:c0 + LANES] = staged[p, src, :].astype(_BF)
        if part > 0:
            def kept_positions():
                for p in range(_PAIRS):
                    c0 = (part - 1) * D_ATT + p * LANES
                    cref[c0:c0 + LANES, :] = scr_ref[k % _QKV_SLOTS, p, tm - min(keep, tm):tm, :].T
            if keep == tm * tiles_per_seq:
                kept_positions()
            else:
                pl.when(last_tile)(kept_positions)

    n_proj = 3 * N_GROUPS
    project(0)
    for k in range(1, n_proj + 1):
        if k < n_proj:
            project(k)
        emit(k - 1)


def _qkv_prompt(x3d, g, w, tm=512):
    batch, seq, d = x3d.shape
    n = w.shape[1]
    tps = seq // tm
    keeps = tuple(min(wd, seq) for wd in WINDOWS)
    out_shapes, out_specs = [], []
    for dil in DILATIONS:
        assert tm % (16 * dil) == 0
        out_shapes.append(jax.ShapeDtypeStruct((batch, dil, seq // dil, 3 * D_ATT), _BF))
        out_specs.append(pl.BlockSpec((None, dil, tm // dil, 3 * D_ATT), lambda b, s: (b, 0, s, 0)))
    for keep in keeps:
        assert keep == seq or keep <= tm
        out_shapes.append(jax.ShapeDtypeStruct((batch, 2 * D_ATT, keep), _F32))
        if keep == seq:
            out_specs.append(pl.BlockSpec((None, 2 * D_ATT, tm), lambda b, s: (b, 0, s)))
        else:
            out_specs.append(pl.BlockSpec((None, 2 * D_ATT, keep), lambda b, s: (b, 0, 0)))
    return pl.pallas_call(
        functools.partial(_qkv_prompt_kernel, tm=tm, tiles_per_seq=tps, keeps=keeps),
        grid=(batch, tps),
        in_specs=[pl.BlockSpec((None, tm, d), lambda b, s: (b, s, 0)), _const_spec((1, d)), _const_spec((d, n))],
        out_specs=out_specs,
        out_shape=out_shapes,
        scratch_shapes=[pltpu.VMEM((_QKV_SLOTS, _PAIRS, tm, LANES), _F32), pltpu.VMEM((tm, d), _BF),
                        pltpu.VMEM((_PAIRS, tm, LANES), _F32)],
        compiler_params=_params(2),
        name="dsw_qkv_prompt",
    )(x3d, g.reshape(1, d), w)


def _dsw_prompt_kernel(qkv_ref, bias_ref, o_ref, lse_ref, *, dil, nb):
    nk = 2 * BLK if nb > 1 else BLK
    qi = lax.broadcasted_iota(jnp.int32, (BLK, nk), 0)
    ki = lax.broadcasted_iota(jnp.int32, (BLK, nk), 1)
    rel = qi + (nk - BLK) - ki
    band = (rel >= 0) & (rel <= BLK)
    lo = lax.broadcasted_iota(jnp.int32, (BLK, LANES), 1) < DH_A

    def block(idx, carry):
        r = idx // nb
        n = idx % nb
        q0 = pl.multiple_of(n * BLK, BLK)
        if nb > 1:
            p0 = pl.multiple_of(jnp.maximum(n - 1, 0) * BLK, BLK)
            valid = band & jnp.logical_or(ki >= BLK, n > 0)
        else:
            valid = band
        start = q0 * dil + r
        rows = pl.ds(start, BLK) if dil == 1 else pl.ds(start, BLK, stride=dil)
        vps, scores = [], []
        for pair in range(_PAIRS):
            cq, ck, cv = pair * LANES, D_ATT + pair * LANES, 2 * D_ATT + pair * LANES
            qp = qkv_ref[r, pl.ds(q0, BLK), cq:cq + LANES]
            kp = qkv_ref[r, pl.ds(q0, BLK), ck:ck + LANES]
            vp = qkv_ref[r, pl.ds(q0, BLK), cv:cv + LANES]
            if nb > 1:
                kp = jnp.concatenate([qkv_ref[r, pl.ds(p0, BLK), ck:ck + LANES], kp], axis=0)
                vp = jnp.concatenate([qkv_ref[r, pl.ds(p0, BLK), cv:cv + LANES], vp], axis=0)
            vps.append(vp)
            for hh in range(2):
                sel = lo if hh == 0 else jnp.logical_not(lo)
                qm = jnp.where(sel, qp, jnp.zeros_like(qp))
                s = lax.dot_general(qm, kp, _NT, preferred_element_type=_F32)
                scores.append(jnp.where(valid, s + bias_ref[pair * 2 + hh], NEG))
        ms = [jnp.max(s, axis=-1, keepdims=True) for s in scores]
        ps = [jnp.exp(s - m) for s, m in zip(scores, ms)]
        ls = [jnp.sum(p, axis=-1, keepdims=True) for p in ps]
        pvs = [jnp.dot(p.astype(_BF), vps[h // 2], preferred_element_type=_F32) for h, p in enumerate(ps)]
        for pair in range(_PAIRS):
            h0, h1 = 2 * pair, 2 * pair + 1
            o_ref[pair, rows, :] = jnp.where(lo, pvs[h0] / ls[h0], pvs[h1] / ls[h1])
            lse_ref[pair, rows, :] = jnp.where(lo, jnp.broadcast_to(ms[h0] + jnp.log(ls[h0]), (BLK, LANES)),
                                               jnp.broadcast_to(ms[h1] + jnp.log(ls[h1]), (BLK, LANES)))
        return carry

    lax.fori_loop(0, dil * nb, block, 0, unroll=4)


def _dsw_prompt_attn(qkv_g, bias_full, g):
    batch, dil, sub, width = qkv_g.shape
    seq = dil * sub
    assert seq % WINDOWS[g] == 0 and dil == DILATIONS[g]
    nb = sub // BLK
    if nb == 1:
        bias_full = bias_full[:, :, BLK:]
    out = jax.ShapeDtypeStruct((batch, _PAIRS, seq, LANES), _F32)
    out_spec = pl.BlockSpec((None, _PAIRS, seq, LANES), lambda b: (b, 0, 0, 0))
    return pl.pallas_call(
        functools.partial(_dsw_prompt_kernel, dil=dil, nb=nb),
        grid=(batch,),
        in_specs=[pl.BlockSpec((None, dil, sub, width), lambda b: (b, 0, 0, 0)), _const_spec(bias_full.shape)],
        out_specs=[out_spec, out_spec],
        out_shape=[out, out],
        compiler_params=_params(1),
        name=f"dsw_prompt_attn_g{g}",
    )(qkv_g, bias_full)


def _merge_groups(os, ls):
    la, lb, lc = ls
    mx = jnp.maximum(jnp.maximum(la, lb), lc)
    ea, eb, ec = jnp.exp(la - mx), jnp.exp(lb - mx), jnp.exp(lc - mx)
    return (ea * os[0] + eb * os[1] + ec * os[2]) / (ea + eb + ec)


def _dsw_merge_kernel(o0_ref, o1_ref, o2_ref, l0_ref, l1_ref, l2_ref, w_ref, x_ref, out_ref):
    o = _merge_groups([o0_ref[...], o1_ref[...], o2_ref[...]], [l0_ref[...], l1_ref[...], l2_ref[...]])
    out_ref[...] = x_ref[...] + jnp.dot(o.astype(_BF), w_ref[...], preferred_element_type=_F32)


def _dsw_merge(outs, lses, w_o, x2d, tm=512):
    m, d = x2d.shape
    tm = min(tm, m)
    spec = pl.BlockSpec((tm, D_ATT), lambda i: (i, 0))
    return pl.pallas_call(
        _dsw_merge_kernel,
        grid=(m // tm,),
        in_specs=[spec] * 6 + [_const_spec((D_ATT, d)), pl.BlockSpec((tm, d), lambda i: (i, 0))],
        out_specs=pl.BlockSpec((tm, d), lambda i: (i, 0)),
        out_shape=jax.ShapeDtypeStruct((m, d), _F32),
        compiler_params=_params(1),
        name="dsw_merge_out",
    )(*outs, *lses, w_o, x2d)


_NT = (((1,), (1,)), ((), ()))


def _linear_t_kernel(x_ref, g_ref, wt_ref, o_ref, *, n_chunk):
    hb = _rms(x_ref[...], g_ref[...]).astype(_BF)
    for c0 in range(0, wt_ref.shape[0], n_chunk):
        o_ref[c0:c0 + n_chunk, :] = lax.dot_general(wt_ref[c0:c0 + n_chunk, :], hb, _NT,
                                                    preferred_element_type=_F32)


def _linear_t(x, norm_g, wt, name):
    m, k = x.shape
    n = wt.shape[0]
    return pl.pallas_call(
        functools.partial(_linear_t_kernel, n_chunk=512),
        grid=(1,),
        in_specs=[_const_spec((m, k)), _const_spec((1, k)), _const_spec((n, k))],
        out_specs=pl.BlockSpec((n, m), lambda i: (0, 0)),
        out_shape=jax.ShapeDtypeStruct((n, m), _F32),
        compiler_params=_params(1),
        name=name,
    )(x, norm_g.reshape(1, k), wt)


def _dsw_sample_kernel(qkv_ref, kvt_ref, cache_ref, bias_ref, b0_ref, o_ref, lse_ref, new_ref, *, g, window,
                       dil, rows):
    b = pl.program_id(0)
    base = g * 3 * D_ATT
    q = qkv_ref[:, base:base + D_ATT] * (DH_A ** -0.5)
    knew = qkv_ref[:, base + D_ATT:base + 2 * D_ATT]
    vnew = qkv_ref[:, base + 2 * D_ATT:base + 3 * D_ATT]
    head = lax.broadcasted_iota(jnp.int32, (H_A, D_ATT), 0)
    lane_head = lax.broadcasted_iota(jnp.int32, (H_A, D_ATT), 1) // DH_A
    hm = head == lane_head
    qrows = jnp.where(hm, jnp.broadcast_to(q, (H_A, D_ATT)), 0.0).astype(_BF)
    s = jnp.dot(qrows, cache_ref[0:D_ATT, :].astype(_BF), preferred_element_type=_F32)
    pos = lax.broadcasted_iota(jnp.int32, (H_A, window), 1)
    s = jnp.where((pos & (dil - 1)) == 0, s + bias_ref[...], NEG)
    prod = q.astype(_BF).astype(_F32) * knew.astype(_BF).astype(_F32)
    s_new = jnp.sum(jnp.where(hm, jnp.broadcast_to(prod, (H_A, D_ATT)), 0.0), axis=-1, keepdims=True) \
        + b0_ref[...]
    m = jnp.maximum(jnp.max(s, axis=-1, keepdims=True), s_new)
    pr = jnp.exp(s - m)
    p_new = jnp.exp(s_new - m)
    l = jnp.sum(pr, axis=-1, keepdims=True) + p_new
    o8 = lax.dot_general(pr.astype(_BF), cache_ref[D_ATT:2 * D_ATT, :].astype(_BF), _NT,
                         preferred_element_type=_F32)
    o8 = (o8 + p_new.astype(_BF).astype(_F32) * vnew.astype(_BF).astype(_F32)) / l
    lse8 = jnp.broadcast_to(m + jnp.log(l), (H_A, D_ATT))
    o_ref[...] = jnp.sum(jnp.where(hm, o8, 0.0), axis=0, keepdims=True)
    lse_ref[...] = jnp.sum(jnp.where(hm, lse8, 0.0), axis=0, keepdims=True)
    mine = lax.broadcasted_iota(jnp.int32, (rows, kvt_ref.shape[1]), 1) == b
    last = lax.broadcasted_iota(jnp.int32, (rows, LANES), 1) == LANES - 1
    for r0 in range(0, 2 * D_ATT, rows):
        rolled = pltpu.roll(cache_ref[r0:r0 + rows, :], window - 1, 1)
        col = jnp.sum(jnp.where(mine, kvt_ref[r0:r0 + rows, :], 0.0), axis=1, keepdims=True)
        if window > LANES:
            new_ref[r0:r0 + rows, :window - LANES] = rolled[:, :window - LANES]
        new_ref[r0:r0 + rows, window - LANES:] = jnp.where(last, col, rolled[:, window - LANES:])


def _dsw_sample(qkv_s, kvt_s, cache_t, bias_g, g):
    db = qkv_s.shape[0]
    window, dil = WINDOWS[g], DILATIONS[g]
    assert cache_t.shape[2] == window and dil & (dil - 1) == 0
    width = qkv_s.shape[-1]
    w = np.arange(window)
    back = np.where(w % dil == 0, (window - w) // dil, 0)
    bias_pos = bias_g[:, back]
    b0 = bias_g[:, 0:1]
    blk = (None, 2 * D_ATT, window)
    o, lse, new = pl.pallas_call(
        functools.partial(_dsw_sample_kernel, g=g, window=window, dil=dil, rows=128),
        grid=(db,),
        in_specs=[pl.BlockSpec((None, 1, width), lambda b: (b, 0, 0)),
                  pl.BlockSpec((None, 2 * D_ATT, db), lambda b: (g, 0, 0)),
                  pl.BlockSpec(blk, lambda b: (b, 0, 0)),
                  _const_spec((H_A, window)), _const_spec((H_A, 1))],
        out_specs=[pl.BlockSpec((None, 1, D_ATT), lambda b: (b, 0, 0)),
                   pl.BlockSpec((None, 1, D_ATT), lambda b: (b, 0, 0)),
                   pl.BlockSpec(blk, lambda b: (b, 0, 0))],
        out_shape=[jax.ShapeDtypeStruct((db, 1, D_ATT), _F32), jax.ShapeDtypeStruct((db, 1, D_ATT), _F32),
                   jax.ShapeDtypeStruct((db, 2 * D_ATT, window), _F32)],
        compiler_params=_params(1),
        name=f"dsw_sample_g{g}",
    )(qkv_s.reshape(db, 1, width), kvt_s, cache_t, bias_pos, b0)
    return o.reshape(db, D_ATT), lse.reshape(db, D_ATT), new


def _ffn_gate(ug, ug1, ug2, uv, uv1, uv2, cw_ref, cb_ref, cg, cv, chunk):
    def conv(u, u1, u2, c0):
        cs = slice(c0, c0 + chunk)
        return cb_ref[:, cs] + cw_ref[0:1, cs] * u2 + cw_ref[1:2, cs] * u1 + cw_ref[2:3, cs] * u
    yg = conv(ug, ug1, ug2, cg)
    yv = conv(uv, uv1, uv2, cv)
    return (yg * (1.0 / (1.0 + jnp.exp(-yg))) * yv).astype(_BF)


def _ffn_prompt_kernel(*refs, ts, d_ff, chunk, final_norm, n_mix):
    x_ref, mix_refs, refs = refs[0], refs[1:1 + n_mix], refs[1 + n_mix:]
    if final_norm:
        st_ref, g_ref, wup_ref, cw_ref, cb_ref, wdn_ref, gf_ref, o_ref, ns_ref, carry_ref, act_ref = refs
    else:
        st_ref, g_ref, wup_ref, cw_ref, cb_ref, wdn_ref, o_ref, ns_ref, carry_ref, act_ref = refs
        gf_ref = None
    s = pl.program_id(1)

    @pl.when(s == 0)
    def _():
        carry_ref[...] = st_ref[...]

    if n_mix == 7:
        o_refs, l_refs, wo_ref = mix_refs[0:3], mix_refs[3:6], mix_refs[6]
        slabs = [_merge_groups([r[p] for r in o_refs], [r[p] for r in l_refs]).astype(_BF)
                 for p in range(_PAIRS)]
        mixed = jnp.concatenate(slabs, axis=1)
    else:
        mixed, wo_ref = mix_refs[0][...], mix_refs[1]
    x = x_ref[...] + jnp.dot(mixed, wo_ref[...], preferred_element_type=_F32)
    hb = _rms(x, g_ref[...]).astype(_BF)
    row = lax.broadcasted_iota(jnp.int32, (8, chunk), 0)
    first, second = row == 0, row == 1

    def up(c0):
        u = jnp.dot(hb, wup_ref[:, c0:c0 + chunk], preferred_element_type=_F32)
        p2 = carry_ref[0:1, c0:c0 + chunk]
        p1 = carry_ref[1:2, c0:c0 + chunk]
        r1, r2 = pltpu.roll(u, 1, 0), pltpu.roll(u, 2, 0)
        u1 = jnp.concatenate([jnp.where(first, p1, r1[:8]), r1[8:]], axis=0)
        u2 = jnp.concatenate([jnp.where(first, p2, jnp.where(second, p1, r2[:8])), r2[8:]], axis=0)
        carry_ref[:, c0:c0 + chunk] = u[ts - 2:ts, :]
        return u, u1, u2

    for c in range(d_ff // chunk):
        cg, cv = c * chunk, d_ff + c * chunk
        act_ref[:, cg:cg + chunk] = _ffn_gate(*up(cg), *up(cv), cw_ref, cb_ref, cg, cv, chunk)
    y = x + jnp.dot(act_ref[...], wdn_ref[...], preferred_element_type=_F32)
    if final_norm:
        y = _rms(y, gf_ref[...])
    o_ref[...] = y

    @pl.when(s == pl.num_programs(1) - 1)
    def _():
        ns_ref[...] = carry_ref[...]


def _ffn_prompt(x3d, mix, w_o, state, g, w_up, conv_w, conv_b, w_down, g_final=None, ts=512, chunk=256):
    batch, seq, d = x3d.shape
    d_ff = w_down.shape[0]
    assert d_ff % chunk == 0 and seq % ts == 0
    final_norm = g_final is not None
    tile = pl.BlockSpec((None, ts, d), lambda b, s: (b, s, 0))
    if isinstance(mix, tuple):
        slab = pl.BlockSpec((None, _PAIRS, ts, LANES), lambda b, s: (b, 0, s, 0))
        mix_args, mix_specs = [*mix[0], *mix[1]], [slab] * 6
    else:
        mix_args, mix_specs = [mix], [pl.BlockSpec((None, ts, mix.shape[-1]), lambda b, s: (b, s, 0))]
    mix_args.append(w_o)
    mix_specs.append(_const_spec(w_o.shape))
    args = [x3d, *mix_args, state, g.reshape(1, d), w_up, conv_w, conv_b.reshape(1, 2 * d_ff), w_down]
    specs = [tile, *mix_specs,
             pl.BlockSpec((None, CONV_W - 1, 2 * d_ff), lambda b, s: (b, 0, 0)),
             _const_spec((1, d)), _const_spec((d, 2 * d_ff)), _const_spec((CONV_W, 2 * d_ff)),
             _const_spec((1, 2 * d_ff)), _const_spec((d_ff, d))]
    if final_norm:
        args.append(g_final.reshape(1, d))
        specs.append(_const_spec((1, d)))
    return pl.pallas_call(
        functools.partial(_ffn_prompt_kernel, ts=ts, d_ff=d_ff, chunk=chunk, final_norm=final_norm,
                          n_mix=len(mix_args)),
        grid=(batch, seq // ts),
        in_specs=specs,
        out_specs=[pl.BlockSpec((None, ts, d), lambda b, s: (b, s, 0)),
                   pl.BlockSpec((None, CONV_W - 1, 2 * d_ff), lambda b, s: (b, 0, 0))],
        out_shape=[jax.ShapeDtypeStruct((batch, seq, d), _F32),
                   jax.ShapeDtypeStruct((batch, CONV_W - 1, 2 * d_ff), _F32)],
        scratch_shapes=[pltpu.VMEM((CONV_W - 1, 2 * d_ff), _F32), pltpu.VMEM((ts, d_ff), _BF)],
        compiler_params=_params(2),
        name="conv_ffn_prompt",
    )(*args)


def _ffn_sample_kernel(*refs, d_ff, chunk, final_norm):
    if final_norm:
        x_ref, p2_ref, p1_ref, g_ref, wup_ref, cw_ref, cb_ref, wdn_ref, gf_ref, o_ref, u_ref, act_ref = refs
    else:
        x_ref, p2_ref, p1_ref, g_ref, wup_ref, cw_ref, cb_ref, wdn_ref, o_ref, u_ref, act_ref = refs
        gf_ref = None
    x = x_ref[...]
    hb = _rms(x, g_ref[...]).astype(_BF)

    def up(c0):
        cs = slice(c0, c0 + chunk)
        u = jnp.dot(hb, wup_ref[:, cs], preferred_element_type=_F32)
        u_ref[:, cs] = u
        return u, p1_ref[:, cs], p2_ref[:, cs]

    for c in range(d_ff // chunk):
        cg, cv = c * chunk, d_ff + c * chunk
        act_ref[:, cg:cg + chunk] = _ffn_gate(*up(cg), *up(cv), cw_ref, cb_ref, cg, cv, chunk)
    y = x + jnp.dot(act_ref[...], wdn_ref[...], preferred_element_type=_F32)
    if final_norm:
        y = _rms(y, gf_ref[...])
    o_ref[...] = y


def _ffn_sample(x2d, prev2, prev1, g, w_up, conv_w, conv_b, w_down, g_final=None, chunk=256):
    db, d = x2d.shape
    d_ff = w_down.shape[0]
    final_norm = g_final is not None
    args = [x2d, prev2, prev1, g.reshape(1, d), w_up, conv_w, conv_b.reshape(1, 2 * d_ff), w_down]
    specs = [_const_spec((db, d)), _const_spec((db, 2 * d_ff)), _const_spec((db, 2 * d_ff)),
             _const_spec((1, d)), _const_spec((d, 2 * d_ff)), _const_spec((CONV_W, 2 * d_ff)),
             _const_spec((1, 2 * d_ff)), _const_spec((d_ff, d))]
    if final_norm:
        args.append(g_final.reshape(1, d))
        specs.append(_const_spec((1, d)))
    return pl.pallas_call(
        functools.partial(_ffn_sample_kernel, d_ff=d_ff, chunk=chunk, final_norm=final_norm),
        grid=(1,),
        in_specs=specs,
        out_specs=[pl.BlockSpec((db, d), lambda i: (0, 0)), pl.BlockSpec((db, 2 * d_ff), lambda i: (0, 0))],
        out_shape=[jax.ShapeDtypeStruct((db, d), _F32), jax.ShapeDtypeStruct((db, 2 * d_ff), _F32)],
        scratch_shapes=[pltpu.VMEM((db, d_ff), _BF)],
        compiler_params=_params(1),
        name="conv_ffn_sample",
    )(*args)


def _mla_proj_kernel(x_ref, ck_ref, sk_ref, g_ref, win_ref, gq_ref, gkv_ref, wq_ref, wk_ref, wv_ref,
                     q_ref, k_ref, v_ref, ckv_ref, kr_ref):
    hb = _rms(x_ref[...], g_ref[...]).astype(_BF)
    proj = jnp.dot(hb, win_ref[...], preferred_element_type=_F32)
    c_q = _rms(proj[:, :Q_LORA], gq_ref[...])
    c_kv = _rms(proj[:, Q_LORA:Q_LORA + KV_LORA], gkv_ref[...])
    ckv_ref[...] = c_kv
    off = Q_LORA + KV_LORA
    ck, sk = ck_ref[...], sk_ref[...]
    kr_pad = proj[:, off:off + LANES] * ck + proj[:, off + LANES:off + 2 * LANES] * sk
    kr_ref[...] = kr_pad[:, QK_NOPE:QK_NOPE + QK_ROPE]
    scale = (QK_NOPE + QK_ROPE) ** -0.5 * LOG2E
    lane = lax.broadcasted_iota(jnp.int32, ck.shape, 1)
    nope = (lane < QK_NOPE).astype(_F32)
    ones_col = (lane == V_HEAD).astype(_F32)
    cq_tab = scale * (ck + nope)
    sq_tab = scale * sk
    cqb = c_q.astype(_BF)
    ckvb = c_kv.astype(_BF)
    hw = H_B * LANES
    for hp in range(H_B // 2):
        sl = slice(2 * hp * LANES, 2 * (hp + 1) * LANES)
        qa = jnp.dot(cqb, wq_ref[:, sl], preferred_element_type=_F32)
        qs = jnp.dot(cqb, wq_ref[:, hw + 2 * hp * LANES:hw + 2 * (hp + 1) * LANES], preferred_element_type=_F32)
        kn = jnp.dot(ckvb, wk_ref[:, sl], preferred_element_type=_F32)
        vv = jnp.dot(ckvb, wv_ref[:, sl], preferred_element_type=_F32)
        for hh in range(2):
            h, one = 2 * hp + hh, slice(hh * LANES, (hh + 1) * LANES)
            q_ref[h] = (qa[:, one] * cq_tab + qs[:, one] * sq_tab).astype(_BF)
            k_ref[h] = (kn[:, one] + kr_pad).astype(_BF)
            v_ref[h] = (vv[:, one] + ones_col).astype(_BF)


def _mla_proj(x2d, ck_tab, sk_tab, g, w_in_ext, g_q, g_kv, w_q_both, w_k_pad, w_v, n_seq, tm):
    m, d = x2d.shape
    rows = m // n_seq
    tm = min(tm, rows)
    tps = rows // tm
    x_map = lambda s, b: (b * tps + s, 0)
    t_map = lambda s, b: (s, 0)
    return pl.pallas_call(
        _mla_proj_kernel,
        grid=(tps, n_seq),
        in_specs=[pl.BlockSpec((tm, d), x_map), pl.BlockSpec((tm, LANES), t_map), pl.BlockSpec((tm, LANES), t_map),
                  _const_spec((1, d)), _const_spec(w_in_ext.shape), _const_spec((1, Q_LORA)),
                  _const_spec((1, KV_LORA)), _const_spec(w_q_both.shape), _const_spec(w_k_pad.shape),
                  _const_spec(w_v.shape)],
        out_specs=[pl.BlockSpec((H_B, tm, LANES), lambda s, b: (0, b * tps + s, 0)),
                   pl.BlockSpec((H_B, tm, LANES), lambda s, b: (0, b * tps + s, 0)),
                   pl.BlockSpec((H_B, tm, LANES), lambda s, b: (0, b * tps + s, 0)),
                   pl.BlockSpec((tm, KV_LORA), x_map), pl.BlockSpec((tm, QK_ROPE), x_map)],
        out_shape=[jax.ShapeDtypeStruct((H_B, m, LANES), _BF), jax.ShapeDtypeStruct((H_B, m, LANES), _BF),
                   jax.ShapeDtypeStruct((H_B, m, LANES), _BF),
                   jax.ShapeDtypeStruct((m, KV_LORA), _F32), jax.ShapeDtypeStruct((m, QK_ROPE), _F32)],
        compiler_params=_params(2),
        name="mla_proj",
    )(x2d, ck_tab, sk_tab, g.reshape(1, d), w_in_ext, g_q.reshape(1, Q_LORA), g_kv.reshape(1, KV_LORA),
      w_q_both, w_k_pad, w_v)


def _mla_prompt_kernel(q_ref, k_ref, v_ref, o_ref, s_ref, *, seq, tq, ck):
    lo = lax.broadcasted_iota(jnp.int32, (tq, LANES), 1) < V_HEAD
    row = lax.broadcasted_iota(jnp.int32, (tq, tq), 0)
    colk = lax.broadcasted_iota(jnp.int32, (tq, tq), 1)
    causal = colk <= row
    units = [(qi, hh) for qi in range(seq // tq) for hh in range(2)]
    state = [dict(mx=None, m=None, acc=None) for _ in units]
    outs = {}

    def chunks_of(qi):
        kv_len = (qi + 1) * tq
        return [(c0, min(ck, kv_len - c0)) for c0 in range(0, kv_len, ck)]

    def score_tasks(u):
        (qi, hh), st, slot = units[u], state[u], u % 2
        q0, kv_len = qi * tq, (qi + 1) * tq

        def task(c0, cw):
            s = lax.dot_general(q_ref[hh, q0:q0 + tq, :], k_ref[hh, c0:c0 + cw, :], _NT,
                                preferred_element_type=_F32)
            if c0 + cw == kv_len:
                sd = jnp.where(causal, s[:, cw - tq:], NEG)
                if cw > tq:
                    s_ref[slot, :, c0:c0 + cw - tq] = s[:, :cw - tq]
                s_ref[slot, :, kv_len - tq:kv_len] = sd
                tiles = [s[:, j:j + LANES] for j in range(0, cw - tq, LANES)]
                tiles += [sd[:, j:j + LANES] for j in range(0, tq, LANES)]
            else:
                s_ref[slot, :, c0:c0 + cw] = s
                tiles = [s[:, j:j + LANES] for j in range(0, cw, LANES)]
            for t in tiles:
                st["mx"] = t if st["mx"] is None else jnp.maximum(st["mx"], t)

        return [functools.partial(task, c0, cw) for c0, cw in chunks_of(qi)]

    def value_tasks(u):
        (qi, hh), st, slot = units[u], state[u], u % 2
        q0 = qi * tq

        def task(c0, cw, first, last):
            if first:
                st["m"] = jnp.max(st["mx"], axis=-1, keepdims=True)
            p = jnp.exp2(s_ref[slot, :, c0:c0 + cw] - st["m"]).astype(_BF)
            pv = jnp.dot(p, v_ref[hh, c0:c0 + cw, :], preferred_element_type=_F32)
            st["acc"] = pv if st["acc"] is None else st["acc"] + pv
            if last:
                outs[hh] = st["acc"] / st["acc"][:, V_HEAD:V_HEAD + 1]
                if hh == 1:
                    o_ref[q0:q0 + tq, :] = jnp.where(lo, outs[0], pltpu.roll(outs[1], V_HEAD, 1)).astype(o_ref.dtype)

        ch = chunks_of(qi)
        return [functools.partial(task, c0, cw, i == 0, i == len(ch) - 1) for i, (c0, cw) in enumerate(ch)]

    pending = []
    for u in range(len(units) + 1):
        scoring = score_tasks(u) if u < len(units) else []
        for i in range(max(len(scoring), len(pending))):
            if i < len(scoring):
                scoring[i]()
            if i < len(pending):
                pending[i]()
        pending = value_tasks(u) if u < len(units) else []


def _mla_prompt_attn(q, k, v, batch, seq, tq=256, ck=512):
    m = batch * seq
    blk = pl.BlockSpec((2, seq, LANES), lambda b, hp: (hp, b, 0))
    return pl.pallas_call(
        functools.partial(_mla_prompt_kernel, seq=seq, tq=tq, ck=ck),
        grid=(batch, H_B // 2),
        in_specs=[blk, blk, blk],
        out_specs=pl.BlockSpec((seq, LANES), lambda b, hp: (b, hp)),
        out_shape=jax.ShapeDtypeStruct((m, H_B * V_HEAD), _BF),
        scratch_shapes=[pltpu.VMEM((2, tq, seq), _F32)],
        compiler_params=_params(2),
        name="mla_prompt_attn",
    )(q, k, v)


def _mla_absorb_kernel(q_ref, wuk_ref, o_ref):
    for h in range(H_B):
        o_ref[h] = jnp.dot(q_ref[h], wuk_ref[h], preferred_element_type=_F32).astype(o_ref.dtype)


def _mla_absorb(q, w_uk_t_pad):
    db = q.shape[1]
    return pl.pallas_call(
        _mla_absorb_kernel,
        grid=(1,),
        in_specs=[_const_spec(q.shape), _const_spec(w_uk_t_pad.shape)],
        out_specs=pl.BlockSpec((H_B, db, KV_LORA), lambda i: (0, 0, 0)),
        out_shape=jax.ShapeDtypeStruct((H_B, db, KV_LORA), _BF),
        compiler_params=_params(1),
        name="mla_absorb_q",
    )(q, w_uk_t_pad)


def _mla_sample_kernel(pt_ref, qlat_ref, qrope_ref, cnew_ref, krnew_ref, ckv_hbm, kr_hbm, o_ref,
                       ckv_buf, kr_buf, sems, *, pages, n_pages):
    b = pl.program_id(0)
    n_chunks = n_pages // pages

    def copies(page_of, slot):
        out = []
        for t in range(pages):
            page = page_of(t)
            out.append(pltpu.make_async_copy(ckv_hbm.at[page], ckv_buf.at[slot, t], sems.at[0, slot]))
            out.append(pltpu.make_async_copy(kr_hbm.at[page], kr_buf.at[slot, t], sems.at[1, slot]))
        return out

    def start(seq, chunk, slot):
        for cp in copies(lambda t: pt_ref[seq * n_pages + chunk * pages + t], slot):
            cp.start()

    def wait(slot):
        for cp in copies(lambda t: 0, slot):
            cp.wait()

    @pl.when(b == 0)
    def _():
        start(0, 0, 0)

    qlat = qlat_ref[...]
    qrope = qrope_ref[...]

    def update(carry, s, weighted_values):
        m_prev, l_prev, acc = carry
        m_new = jnp.maximum(m_prev, jnp.max(s, axis=-1, keepdims=True))
        alpha = jnp.exp2(m_prev - m_new)
        p = jnp.exp2(s - m_new)
        return (m_new, alpha * l_prev + jnp.sum(p, axis=-1, keepdims=True),
                alpha * acc + weighted_values(p.astype(_BF)))

    carry = (jnp.full((H_B, 1), -jnp.inf, _F32), jnp.zeros((H_B, 1), _F32), jnp.zeros((H_B, KV_LORA), _F32))
    for c in range(n_chunks):
        slot = c % 2
        if c + 1 < n_chunks:
            start(b, c + 1, 1 - slot)
        else:
            @pl.when(b + 1 < pl.num_programs(0))
            def _():
                start(b + 1, 0, 1 - slot)
        wait(slot)
        ckv = ckv_buf[slot].reshape(pages * ckv_buf.shape[2], KV_LORA).astype(_BF)
        s_rope = jnp.concatenate(
            [jnp.dot(qrope, kr_buf[slot, t].astype(_BF), preferred_element_type=_F32) for t in range(pages)],
            axis=1)
        s = lax.dot_general(qlat, ckv, _NT, preferred_element_type=_F32) + s_rope
        carry = update(carry, s, lambda pb, ckv=ckv: jnp.dot(pb, ckv, preferred_element_type=_F32))

    cn = cnew_ref[...].astype(_BF).astype(_F32)
    krn = krnew_ref[...].astype(_BF).astype(_F32)
    s = jnp.sum(qlat.astype(_F32) * cn, axis=-1, keepdims=True) \
        + jnp.sum(qrope.astype(_F32) * krn, axis=-1, keepdims=True)
    _, l, acc = update(carry, s, lambda pb: pb.astype(_F32) * cn)
    o_ref[...] = (acc / l).astype(o_ref.dtype)


def _mla_sample_attn(page_table, qlat, qrope, c_new, kr_new, ckv_pool, kr_pool_t, pages=32):
    db, n_pages = page_table.shape
    page = ckv_pool.shape[1]
    pages = min(pages, n_pages // 2)
    assert n_pages % (2 * pages) == 0
    per_b = lambda b, pt: (b, 0, 0)
    grid_spec = pltpu.PrefetchScalarGridSpec(
        num_scalar_prefetch=1,
        grid=(db,),
        in_specs=[pl.BlockSpec((None, H_B, KV_LORA), per_b), pl.BlockSpec((None, H_B, QK_ROPE), per_b),
                  pl.BlockSpec((None, 1, KV_LORA), per_b), pl.BlockSpec((None, 1, QK_ROPE), per_b),
                  pl.BlockSpec(memory_space=pl.ANY), pl.BlockSpec(memory_space=pl.ANY)],
        out_specs=pl.BlockSpec((None, H_B, KV_LORA), per_b),
        scratch_shapes=[pltpu.VMEM((2, pages, page, KV_LORA), _F32), pltpu.VMEM((2, pages, QK_ROPE, page), _F32),
                        pltpu.SemaphoreType.DMA((2, 2))],
    )
    return pl.pallas_call(
        functools.partial(_mla_sample_kernel, pages=pages, n_pages=n_pages),
        grid_spec=grid_spec,
        out_shape=jax.ShapeDtypeStruct((db, H_B, KV_LORA), _BF),
        compiler_params=_params(1),
        name="mla_sample_attn",
    )(page_table.reshape(-1), qlat, qrope, c_new, kr_new, ckv_pool, kr_pool_t)


def _mla_unabsorb_kernel(o_ref, wuv_ref, out_ref):
    for p in range(H_B // 2):
        y = jnp.dot(o_ref[2 * p], wuv_ref[2 * p], preferred_element_type=_F32) \
            + jnp.dot(o_ref[2 * p + 1], wuv_ref[2 * p + 1], preferred_element_type=_F32)
        out_ref[:, p * LANES:(p + 1) * LANES] = y.astype(out_ref.dtype)


def _mla_unabsorb(o_lat_hm, w_uv_pair):
    db = o_lat_hm.shape[1]
    return pl.pallas_call(
        _mla_unabsorb_kernel,
        grid=(1,),
        in_specs=[_const_spec(o_lat_hm.shape), _const_spec(w_uv_pair.shape)],
        out_specs=pl.BlockSpec((db, H_B * V_HEAD), lambda i: (0, 0)),
        out_shape=jax.ShapeDtypeStruct((db, H_B * V_HEAD), _BF),
        compiler_params=_params(1),
        name="mla_unabsorb_o",
    )(o_lat_hm, w_uv_pair)


def _t5_bucket(dist):
    n = np.asarray(dist)
    max_exact = N_BUCKETS // 2
    large = max_exact + (np.log(np.maximum(n, 1) / max_exact) / np.log(T5_MAX_DISTANCE / max_exact)
                         * (N_BUCKETS - max_exact)).astype(np.int32)
    large = np.minimum(large, N_BUCKETS - 1)
    return np.where(n < max_exact, n, large).astype(np.int32)


def _dsw_biases(table):
    out = []
    for g, (w, d) in enumerate(zip(WINDOWS, DILATIONS)):
        buckets = _t5_bucket(np.arange(w // d + 1) * d)
        out.append(table[buckets][:, g * H_A:(g + 1) * H_A].T.astype(_F32))
    return out


def _dsw_bias_blocks(bias_g):
    n = 3 * BLK
    diag = bias_g[:, np.clip(2 * BLK - 1 - np.arange(n), 0, BLK)]
    skew = jnp.tile(diag, (1, BLK))[:, :BLK * (n - 1)].reshape(bias_g.shape[0], BLK, n - 1)
    return skew[:, :, BLK - 1:3 * BLK - 1]


def _rope_pad_tables(pos):
    inv = jnp.asarray(ROPE_BASE ** (-np.arange(0, QK_ROPE, 2) / QK_ROPE), dtype=_F32)
    ang = pos.astype(_F32)[:, None] * inv[None, :]
    cos, sin = jnp.cos(ang), jnp.sin(ang)
    n = pos.shape[0]
    zl = jnp.zeros((n, QK_NOPE), _F32)
    zr = jnp.zeros((n, LANES - QK_NOPE - QK_ROPE), _F32)
    return (jnp.concatenate([zl, cos, cos, zr], axis=1), jnp.concatenate([zl, -sin, sin, zr], axis=1))


def _mla_weights(w_in, w_q, w_kv):
    d = w_in.shape[0]
    half = QK_ROPE // 2
    off = Q_LORA + KV_LORA
    kr = w_in[:, off:]
    kr_sw = jnp.concatenate([kr[:, half:], kr[:, :half]], axis=1)
    zl = jnp.zeros((d, QK_NOPE), w_in.dtype)
    zr = jnp.zeros((d, LANES - QK_NOPE - QK_ROPE), w_in.dtype)
    w_in_ext = jnp.concatenate([w_in[:, :off], zl, kr, zr, zl, kr_sw, zr], axis=1).astype(_BF)

    dh = QK_NOPE + QK_ROPE
    wq3 = w_q.reshape(Q_LORA, H_B, dh)
    pad = jnp.zeros((Q_LORA, H_B, LANES - dh), w_q.dtype)
    wq_pad = jnp.concatenate([wq3, pad], axis=2)
    wq_sw = jnp.concatenate([jnp.zeros((Q_LORA, H_B, QK_NOPE), w_q.dtype), wq3[:, :, QK_NOPE + half:],
                             wq3[:, :, QK_NOPE:QK_NOPE + half], pad], axis=2)
    w_q_both = jnp.concatenate([wq_pad.reshape(Q_LORA, H_B * LANES), wq_sw.reshape(Q_LORA, H_B * LANES)],
                               axis=1).astype(_BF)

    wkv3 = w_kv.reshape(KV_LORA, H_B, QK_NOPE + V_HEAD)
    w_uk, w_uv = wkv3[:, :, :QK_NOPE], wkv3[:, :, QK_NOPE:]
    w_k_pad = jnp.concatenate([w_uk, jnp.zeros((KV_LORA, H_B, LANES - QK_NOPE), w_kv.dtype)], axis=2)
    w_k_pad = w_k_pad.reshape(KV_LORA, H_B * LANES).astype(_BF)
    w_v = jnp.concatenate([w_uv, jnp.zeros((KV_LORA, H_B, LANES - V_HEAD), w_kv.dtype)], axis=2)
    w_v = w_v.reshape(KV_LORA, H_B * LANES).astype(_BF)
    w_uk_t = jnp.transpose(w_uk, (1, 2, 0))
    w_uk_t_pad = jnp.concatenate([w_uk_t, jnp.zeros((H_B, LANES - QK_NOPE, KV_LORA), w_kv.dtype)], axis=1)
    w_uv_h = jnp.transpose(w_uv, (1, 0, 2))
    zero = jnp.zeros_like(w_uv_h)
    even = jnp.concatenate([w_uv_h, zero], axis=2)
    odd = jnp.concatenate([zero, w_uv_h], axis=2)
    is_even = (jnp.arange(H_B) % 2 == 0)[:, None, None]
    w_uv_pair = jnp.where(is_even, even, odd)
    return w_in_ext, w_q_both, w_k_pad, w_v, w_uk_t_pad.astype(_BF), w_uv_pair.astype(_BF)


def kernel(x_prompt, x_sample, cache_dsw_g0, cache_dsw_g1, cache_dsw_g2, cache_mla_ckv, cache_mla_kr,
           state_ffn_conv, page_table, rel_bias_table, norm_mix, norm_ffn, norm_final, w_qkv_dsw, w_o_dsw,
           w_in_mla, g_q_mla, g_kv_mla, w_q_mla, w_kv_mla, w_o_mla, w_up_ffn, conv_w_ffn, conv_b_ffn, w_down_ffn):
    batch, seq, d = x_prompt.shape
    db, t_new, _ = x_sample.shape
    assert t_new == 1
    depth = norm_mix.shape[0]
    d_ff = w_down_ffn.shape[1]
    past_len = page_table.shape[1] * cache_mla_ckv.shape[2]
    m = batch * seq
    dsw_caches = (cache_dsw_g0, cache_dsw_g1, cache_dsw_g2)
    biases = _dsw_biases(rel_bias_table)
    bias_blocks = [_dsw_bias_blocks(bg) for bg in biases]

    xp = x_prompt.reshape(m, d)
    xs = x_sample.reshape(db, d)
    dsw_new_p = [[] for _ in range(N_GROUPS)]
    dsw_new_s = [[] for _ in range(N_GROUPS)]
    ckv_p, ckv_s, kr_p, kr_s, conv_p, conv_s = [], [], [], [], [], []
    zero_state = jnp.zeros((batch, CONV_W - 1, 2 * d_ff), _F32)

    for i in range(depth):
        if i % 2 == 0:
            a = i // 2
            w_qkv = w_qkv_dsw[a].astype(_BF)
            w_o = w_o_dsw[a].astype(_BF)
            *qkv_groups, c0, c1, c2 = _qkv_prompt(xp.reshape(batch, seq, d), norm_mix[i], w_qkv)
            for g, c in enumerate((c0, c1, c2)):
                c = c.reshape(batch, 2, H_A, DH_A, min(WINDOWS[g], seq))
                dsw_new_p[g].append(jnp.transpose(c, (0, 4, 1, 2, 3)))
            mix_p = tuple(zip(*[_dsw_prompt_attn(qkv_groups[g], bias_blocks[g], g) for g in range(N_GROUPS)]))
            w_o_p = w_o
            qkv_s = _linear(xs, w_qkv, norm_g=norm_mix[i], name="dsw_qkv_sample")
            w_kv_t = jnp.transpose(w_qkv.reshape(d, N_GROUPS, 3, D_ATT)[:, :, 1:], (1, 2, 3, 0))
            kvt_s = _linear_t(xs, norm_mix[i], w_kv_t.reshape(N_GROUPS * 2 * D_ATT, d), "dsw_kv_sample_t")
            kvt_s = kvt_s.reshape(N_GROUPS, 2 * D_ATT, db)
            outs, lses = [], []
            for g in range(N_GROUPS):
                cache_t = jnp.transpose(dsw_caches[g][a], (0, 2, 3, 4, 1)).reshape(db, 2 * D_ATT, WINDOWS[g])
                o, lse, new = _dsw_sample(qkv_s, kvt_s, cache_t, biases[g], g)
                outs.append(o)
                lses.append(lse)
                new = new.reshape(db, 2, H_A, DH_A, WINDOWS[g])
                dsw_new_s[g].append(jnp.transpose(new, (0, 4, 1, 2, 3)))
            xs = _dsw_merge(outs, lses, w_o, xs)
        else:
            b = i // 2
            w_in_ext, w_q_both, w_k_pad, w_v, w_uk_t_pad, w_uv_pair = _mla_weights(
                w_in_mla[b], w_q_mla[b], w_kv_mla[b])
            w_o = w_o_mla[b].astype(_BF)
            ck_tab, sk_tab = _rope_pad_tables(jnp.arange(seq))
            q, k, v, ckv, kr = _mla_proj(xp, ck_tab, sk_tab, norm_mix[i], w_in_ext, g_q_mla[b], g_kv_mla[b],
                                         w_q_both, w_k_pad, w_v, n_seq=batch, tm=512)
            ckv_p.append(ckv.reshape(batch, seq, KV_LORA))
            kr_p.append(kr.reshape(batch, seq, QK_ROPE))
            mix_p = _mla_prompt_attn(q, k, v, batch, seq).reshape(batch, seq, H_B * V_HEAD)
            w_o_p = w_o
            ck_s, sk_s = _rope_pad_tables(jnp.full((db,), past_len, jnp.int32))
            q, _, _, ckv, kr = _mla_proj(xs, ck_s, sk_s, norm_mix[i], w_in_ext, g_q_mla[b], g_kv_mla[b],
                                         w_q_both, w_k_pad, w_v, n_seq=1, tm=db)
            ckv_s.append(ckv.reshape(db, 1, KV_LORA))
            kr_s.append(kr.reshape(db, 1, QK_ROPE))
            qlat = jnp.transpose(_mla_absorb(q, w_uk_t_pad), (1, 0, 2))
            qrope = jnp.transpose(q[:, :, QK_NOPE:QK_NOPE + QK_ROPE], (1, 0, 2))
            o_lat = _mla_sample_attn(page_table, qlat, qrope, ckv.reshape(db, 1, KV_LORA),
                                     kr.reshape(db, 1, QK_ROPE), cache_mla_ckv[b],
                                     jnp.transpose(cache_mla_kr[b], (0, 2, 1)))
            o = _mla_unabsorb(jnp.transpose(o_lat, (1, 0, 2)), w_uv_pair)
            xs = _linear(o, w_o, residual=xs, name="mla_out_sample")

        g_final = norm_final if i == depth - 1 else None
        w_up = w_up_ffn[i].astype(_BF)
        w_dn = w_down_ffn[i].astype(_BF)
        yp, cp = _ffn_prompt(xp.reshape(batch, seq, d), mix_p, w_o_p, zero_state, norm_ffn[i], w_up,
                             conv_w_ffn[i], conv_b_ffn[i], w_dn, g_final)
        xp = yp.reshape(m, d)
        conv_p.append(cp)
        prev2, prev1 = state_ffn_conv[i, :, 0], state_ffn_conv[i, :, 1]
        xs, u_s = _ffn_sample(xs, prev2, prev1, norm_ffn[i], w_up, conv_w_ffn[i], conv_b_ffn[i], w_dn, g_final)
        conv_s.append(jnp.stack([prev1, u_s], axis=1))

    def stack(per_layer):
        return per_layer[0][None] if len(per_layer) == 1 else jnp.stack(per_layer)

    y_prompt = xp.reshape(batch, seq, d)
    y_sample = xs.reshape(db, 1, d)
    return (y_prompt, y_sample,
            stack(dsw_new_p[0]), stack(dsw_new_s[0]), stack(dsw_new_p[1]), stack(dsw_new_s[1]),
            stack(dsw_new_p[2]), stack(dsw_new_s[2]),
            stack(ckv_p), stack(ckv_s), stack(kr_p), stack(kr_s), stack(conv_p), stack(conv_s))
```

```python
import functools

import numpy as np
import jax
import jax.numpy as jnp
from jax import lax
from jax.experimental import pallas as pl
from jax.experimental.pallas import tpu as pltpu

WINDOWS = (128, 512, 2048)
DILATIONS = (1, 4, 16)
N_GROUPS = 3
H_A = 8
DH_A = 64
D_ATT = H_A * DH_A
BLK = 128
N_BUCKETS = 32
T5_MAX_DISTANCE = 2048
H_B = 16
Q_LORA = 384
KV_LORA = 256
QK_NOPE = 64
QK_ROPE = 32
V_HEAD = 64
ROPE_BASE = 10000.0
CONV_W = 3
EPS = 1e-6
NEG = -1e30
LOG2E = 1.4426950408889634

LANES = 128
VMEM_LIMIT = 56 * 1024 * 1024

_BF = jnp.bfloat16
_F32 = jnp.float32


def _params(n_axes, vmem=VMEM_LIMIT):
    return pltpu.CompilerParams(dimension_semantics=("arbitrary",) * n_axes, vmem_limit_bytes=vmem)


def _const_spec(shape):
    nd = len(shape)
    return pl.BlockSpec(shape, lambda *_: (0,) * nd, pipeline_mode=pl.Buffered(1))


def _rms(x, g):
    return x * lax.rsqrt(jnp.mean(x * x, axis=-1, keepdims=True) + EPS) * g


def _linear_kernel(*refs, has_norm, has_res, n_chunk):
    it = iter(refs)
    x_ref = next(it)
    g_ref = next(it) if has_norm else None
    w_ref = next(it)
    r_ref = next(it) if has_res else None
    o_ref = next(it)
    x = x_ref[...]
    if has_norm:
        x = _rms(x.astype(_F32), g_ref[...])
    xb = x.astype(_BF)
    n = w_ref.shape[1]
    for c0 in range(0, n, n_chunk):
        y = jnp.dot(xb, w_ref[:, c0:c0 + n_chunk], preferred_element_type=_F32)
        if has_res:
            y = y + r_ref[:, c0:c0 + n_chunk]
        o_ref[:, c0:c0 + n_chunk] = y.astype(o_ref.dtype)


def _linear(x, w, *, norm_g=None, residual=None, out_dtype=_F32, tm=512, name="linear"):
    m, k = x.shape
    n = w.shape[1]
    tm = min(tm, m)
    assert m % tm == 0
    n_chunk = 512 if n % 512 == 0 else n
    args, specs = [x], [pl.BlockSpec((tm, k), lambda i: (i, 0))]
    if norm_g is not None:
        args.append(norm_g.reshape(1, k))
        specs.append(_const_spec((1, k)))
    args.append(w)
    specs.append(_const_spec((k, n)))
    if residual is not None:
        args.append(residual)
        specs.append(pl.BlockSpec((tm, n), lambda i: (i, 0)))
    return pl.pallas_call(
        functools.partial(_linear_kernel, has_norm=norm_g is not None, has_res=residual is not None,
                          n_chunk=n_chunk),
        grid=(m // tm,),
        in_specs=specs,
        out_specs=pl.BlockSpec((tm, n), lambda i: (i, 0)),
        out_shape=jax.ShapeDtypeStruct((m, n), out_dtype),
        compiler_params=_params(1),
        name=name,
    )(*args)


_PAIRS = H_A // 2
_QKV_SLOTS = 3


def _qkv_prompt_kernel(x_ref, g_ref, w_ref, q0_ref, q1_ref, q2_ref, c0_ref, c1_ref, c2_ref, scr_ref, hb_ref,
                       scr2_ref, *, tm, tiles_per_seq, keeps):
    last_tile = pl.program_id(1) == tiles_per_seq - 1
    hb_ref[...] = _rms(x_ref[...], g_ref[...]).astype(_BF)
    out_refs = (q0_ref, q1_ref, q2_ref)
    cache_refs = (c0_ref, c1_ref, c2_ref)

    def project(k):
        for half in range(2):
            c0 = k * D_ATT + half * 2 * LANES
            y = jnp.dot(hb_ref[...], w_ref[:, c0:c0 + 2 * LANES], preferred_element_type=_F32)
            if k % 3 == 0:
                y = y * (DH_A ** -0.5)
            for p in range(2):
                scr_ref[k % _QKV_SLOTS, 2 * half + p] = y[:, p * LANES:(p + 1) * LANES]

    def emit(k):
        g, part = divmod(k, 3)
        dil = DILATIONS[g]
        rows = tm // dil
        oref, cref, keep = out_refs[g], cache_refs[g], keeps[g]
        staged, fine = scr_ref.at[k % _QKV_SLOTS], dil
        if dil > 4:
            fine, grp = dil // 4, tm // 4
            for r4 in range(4):
                for p in range(_PAIRS):
                    scr2_ref[p, r4 * grp:(r4 + 1) * grp, :] = staged[p, pl.ds(r4, grp, stride=4), :]
            staged = scr2_ref
        for r in range(dil):
            for p in range(_PAIRS):
                c0 = part * D_ATT + p * LANES
                if dil == 1:
                    src = pl.ds(0, tm)
                elif dil > 4:
                    src = pl.ds((r % 4) * grp + r // 4, rows, stride=fine)
                else:
                    src = pl.ds(r, rows, stride=dil)
                oref[r, :, c0:c0 + LANES] = staged[p, src, :].astype(_BF)
        if part > 0:
            def kept_positions():
                for p in range(_PAIRS):
                    c0 = (part - 1) * D_ATT + p * LANES
                    cref[c0:c0 + LANES, :] = scr_ref[k % _QKV_SLOTS, p, tm - min(keep, tm):tm, :].T
            if keep == tm * tiles_per_seq:
                kept_positions()
            else:
                pl.when(last_tile)(kept_positions)

    n_proj = 3 * N_GROUPS
    project(0)
    for k in range(1, n_proj + 1):
        if k < n_proj:
            project(k)
        emit(k - 1)


def _qkv_prompt(x3d, g, w, tm=512):
    batch, seq, d = x3d.shape
    n = w.shape[1]
    tps = seq // tm
    keeps = tuple(min(wd, seq) for wd in WINDOWS)
    out_shapes, out_specs = [], []
    for dil in DILATIONS:
        assert tm % (16 * dil) == 0
        out_shapes.append(jax.ShapeDtypeStruct((batch, dil, seq // dil, 3 * D_ATT), _BF))
        out_specs.append(pl.BlockSpec((None, dil, tm // dil, 3 * D_ATT), lambda b, s: (b, 0, s, 0)))
    for keep in keeps:
        assert keep == seq or keep <= tm
        out_shapes.append(jax.ShapeDtypeStruct((batch, 2 * D_ATT, keep), _F32))
        if keep == seq:
            out_specs.append(pl.BlockSpec((None, 2 * D_ATT, tm), lambda b, s: (b, 0, s)))
        else:
            out_specs.append(pl.BlockSpec((None, 2 * D_ATT, keep), lambda b, s: (b, 0, 0)))
    return pl.pallas_call(
        functools.partial(_qkv_prompt_kernel, tm=tm, tiles_per_seq=tps, keeps=keeps),
        grid=(batch, tps),
        in_specs=[pl.BlockSpec((None, tm, d), lambda b, s: (b, s, 0)), _const_spec((1, d)), _const_spec((d, n))],
        out_specs=out_specs,
        out_shape=out_shapes,
        scratch_shapes=[pltpu.VMEM((_QKV_SLOTS, _PAIRS, tm, LANES), _F32), pltpu.VMEM((tm, d), _BF),
                        pltpu.VMEM((_PAIRS, tm, LANES), _F32)],
        compiler_params=_params(2),
        name="dsw_qkv_prompt",
    )(x3d, g.reshape(1, d), w)


def _dsw_prompt_kernel(qkv_ref, bias_ref, o_ref, lse_ref, *, dil, nb):
    nk = 2 * BLK if nb > 1 else BLK
    qi = lax.broadcasted_iota(jnp.int32, (BLK, nk), 0)
    ki = lax.broadcasted_iota(jnp.int32, (BLK, nk), 1)
    rel = qi + (nk - BLK) - ki
    band = (rel >= 0) & (rel <= BLK)
    lo = lax.broadcasted_iota(jnp.int32, (BLK, LANES), 1) < DH_A

    def block(idx, carry):
        r = idx // nb
        n = idx % nb
        q0 = pl.multiple_of(n * BLK, BLK)
        if nb > 1:
            p0 = pl.multiple_of(jnp.maximum(n - 1, 0) * BLK, BLK)
            valid = band & jnp.logical_or(ki >= BLK, n > 0)
        else:
            valid = band
        start = q0 * dil + r
        rows = pl.ds(start, BLK) if dil == 1 else pl.ds(start, BLK, stride=dil)
        vps, scores = [], []
        for pair in range(_PAIRS):
            cq, ck, cv = pair * LANES, D_ATT + pair * LANES, 2 * D_ATT + pair * LANES
            qp = qkv_ref[r, pl.ds(q0, BLK), cq:cq + LANES]
            kp = qkv_ref[r, pl.ds(q0, BLK), ck:ck + LANES]
            vp = qkv_ref[r, pl.ds(q0, BLK), cv:cv + LANES]
            if nb > 1:
                kp = jnp.concatenate([qkv_ref[r, pl.ds(p0, BLK), ck:ck + LANES], kp], axis=0)
                vp = jnp.concatenate([qkv_ref[r, pl.ds(p0, BLK), cv:cv + LANES], vp], axis=0)
            vps.append(vp)
            for hh in range(2):
                sel = lo if hh == 0 else jnp.logical_not(lo)
                qm = jnp.where(sel, qp, jnp.zeros_like(qp))
                s = lax.dot_general(qm, kp, _NT, preferred_element_type=_F32)
                scores.append(jnp.where(valid, s + bias_ref[pair * 2 + hh], NEG))
        ms = [jnp.max(s, axis=-1, keepdims=True) for s in scores]
        ps = [jnp.exp(s - m) for s, m in zip(scores, ms)]
        ls = [jnp.sum(p, axis=-1, keepdims=True) for p in ps]
        pvs = [jnp.dot(p.astype(_BF), vps[h // 2], preferred_element_type=_F32) for h, p in enumerate(ps)]
        for pair in range(_PAIRS):
            h0, h1 = 2 * pair, 2 * pair + 1
            o_ref[pair, rows, :] = jnp.where(lo, pvs[h0] / ls[h0], pvs[h1] / ls[h1])
            lse_ref[pair, rows, :] = jnp.where(lo, jnp.broadcast_to(ms[h0] + jnp.log(ls[h0]), (BLK, LANES)),
                                               jnp.broadcast_to(ms[h1] + jnp.log(ls[h1]), (BLK, LANES)))
        return carry

    lax.fori_loop(0, dil * nb, block, 0, unroll=4)


def _dsw_prompt_attn(qkv_g, bias_full, g):
    batch, dil, sub, width = qkv_g.shape
    seq = dil * sub
    assert seq % WINDOWS[g] == 0 and dil == DILATIONS[g]
    nb = sub // BLK
    if nb == 1:
        bias_full = bias_full[:, :, BLK:]
    out = jax.ShapeDtypeStruct((batch, _PAIRS, seq, LANES), _F32)
    out_spec = pl.BlockSpec((None, _PAIRS, seq, LANES), lambda b: (b, 0, 0, 0))
    return pl.pallas_call(
        functools.partial(_dsw_prompt_kernel, dil=dil, nb=nb),
        grid=(batch,),
        in_specs=[pl.BlockSpec((None, dil, sub, width), lambda b: (b, 0, 0, 0)), _const_spec(bias_full.shape)],
        out_specs=[out_spec, out_spec],
        out_shape=[out, out],
        compiler_params=_params(1),
        name=f"dsw_prompt_attn_g{g}",
    )(qkv_g, bias_full)


def _merge_groups(os, ls):
    la, lb, lc = ls
    mx = jnp.maximum(jnp.maximum(la, lb), lc)
    ea, eb, ec = jnp.exp(la - mx), jnp.exp(lb - mx), jnp.exp(lc - mx)
    return (ea * os[0] + eb * os[1] + ec * os[2]) / (ea + eb + ec)


def _dsw_merge_kernel(o0_ref, o1_ref, o2_ref, l0_ref, l1_ref, l2_ref, w_ref, x_ref, out_ref):
    o = _merge_groups([o0_ref[...], o1_ref[...], o2_ref[...]], [l0_ref[...], l1_ref[...], l2_ref[...]])
    out_ref[...] = x_ref[...] + jnp.dot(o.astype(_BF), w_ref[...], preferred_element_type=_F32)


def _dsw_merge(outs, lses, w_o, x2d, tm=512):
    m, d = x2d.shape
    tm = min(tm, m)
    spec = pl.BlockSpec((tm, D_ATT), lambda i: (i, 0))
    return pl.pallas_call(
        _dsw_merge_kernel,
        grid=(m // tm,),
        in_specs=[spec] * 6 + [_const_spec((D_ATT, d)), pl.BlockSpec((tm, d), lambda i: (i, 0))],
        out_specs=pl.BlockSpec((tm, d), lambda i: (i, 0)),
        out_shape=jax.ShapeDtypeStruct((m, d), _F32),
        compiler_params=_params(1),
        name="dsw_merge_out",
    )(*outs, *lses, w_o, x2d)


_NT = (((1,), (1,)), ((), ()))


def _linear_t_kernel(x_ref, g_ref, wt_ref, o_ref, *, n_chunk):
    hb = _rms(x_ref[...], g_ref[...]).astype(_BF)
    for c0 in range(0, wt_ref.shape[0], n_chunk):
        o_ref[c0:c0 + n_chunk, :] = lax.dot_general(wt_ref[c0:c0 + n_chunk, :], hb, _NT,
                                                    preferred_element_type=_F32)


def _linear_t(x, norm_g, wt, name):
    m, k = x.shape
    n = wt.shape[0]
    return pl.pallas_call(
        functools.partial(_linear_t_kernel, n_chunk=512),
        grid=(1,),
        in_specs=[_const_spec((m, k)), _const_spec((1, k)), _const_spec((n, k))],
        out_specs=pl.BlockSpec((n, m), lambda i: (0, 0)),
        out_shape=jax.ShapeDtypeStruct((n, m), _F32),
        compiler_params=_params(1),
        name=name,
    )(x, norm_g.reshape(1, k), wt)


def _dsw_sample_kernel(qkv_ref, kvt_ref, cache_ref, bias_ref, b0_ref, o_ref, lse_ref, new_ref, *, g, window,
                       dil, rows):
    b = pl.program_id(0)
    base = g * 3 * D_ATT
    q = qkv_ref[:, base:base + D_ATT] * (DH_A ** -0.5)
    knew = qkv_ref[:, base + D_ATT:base + 2 * D_ATT]
    vnew = qkv_ref[:, base + 2 * D_ATT:base + 3 * D_ATT]
    head = lax.broadcasted_iota(jnp.int32, (H_A, D_ATT), 0)
    lane_head = lax.broadcasted_iota(jnp.int32, (H_A, D_ATT), 1) // DH_A
    hm = head == lane_head
    qrows = jnp.where(hm, jnp.broadcast_to(q, (H_A, D_ATT)), 0.0).astype(_BF)
    s = jnp.dot(qrows, cache_ref[0:D_ATT, :].astype(_BF), preferred_element_type=_F32)
    pos = lax.broadcasted_iota(jnp.int32, (H_A, window), 1)
    s = jnp.where((pos & (dil - 1)) == 0, s + bias_ref[...], NEG)
    prod = q.astype(_BF).astype(_F32) * knew.astype(_BF).astype(_F32)
    s_new = jnp.sum(jnp.where(hm, jnp.broadcast_to(prod, (H_A, D_ATT)), 0.0), axis=-1, keepdims=True) \
        + b0_ref[...]
    m = jnp.maximum(jnp.max(s, axis=-1, keepdims=True), s_new)
    pr = jnp.exp(s - m)
    p_new = jnp.exp(s_new - m)
    l = jnp.sum(pr, axis=-1, keepdims=True) + p_new
    o8 = lax.dot_general(pr.astype(_BF), cache_ref[D_ATT:2 * D_ATT, :].astype(_BF), _NT,
                         preferred_element_type=_F32)
    o8 = (o8 + p_new.astype(_BF).astype(_F32) * vnew.astype(_BF).astype(_F32)) / l
    lse8 = jnp.broadcast_to(m + jnp.log(l), (H_A, D_ATT))
    o_ref[...] = jnp.sum(jnp.where(hm, o8, 0.0), axis=0, keepdims=True)
    lse_ref[...] = jnp.sum(jnp.where(hm, lse8, 0.0), axis=0, keepdims=True)
    mine = lax.broadcasted_iota(jnp.int32, (rows, kvt_ref.shape[1]), 1) == b
    last = lax.broadcasted_iota(jnp.int32, (rows, LANES), 1) == LANES - 1
    for r0 in range(0, 2 * D_ATT, rows):
        rolled = pltpu.roll(cache_ref[r0:r0 + rows, :], window - 1, 1)
        col = jnp.sum(jnp.where(mine, kvt_ref[r0:r0 + rows, :], 0.0), axis=1, keepdims=True)
        if window > LANES:
            new_ref[r0:r0 + rows, :window - LANES] = rolled[:, :window - LANES]
        new_ref[r0:r0 + rows, window - LANES:] = jnp.where(last, col, rolled[:, window - LANES:])


def _dsw_sample(qkv_s, kvt_s, cache_t, bias_g, g):
    db = qkv_s.shape[0]
    window, dil = WINDOWS[g], DILATIONS[g]
    assert cache_t.shape[2] == window and dil & (dil - 1) == 0
    width = qkv_s.shape[-1]
    w = np.arange(window)
    back = np.where(w % dil == 0, (window - w) // dil, 0)
    bias_pos = bias_g[:, back]
    b0 = bias_g[:, 0:1]
    blk = (None, 2 * D_ATT, window)
    o, lse, new = pl.pallas_call(
        functools.partial(_dsw_sample_kernel, g=g, window=window, dil=dil, rows=128),
        grid=(db,),
        in_specs=[pl.BlockSpec((None, 1, width), lambda b: (b, 0, 0)),
                  pl.BlockSpec((None, 2 * D_ATT, db), lambda b: (g, 0, 0)),
                  pl.BlockSpec(blk, lambda b: (b, 0, 0)),
                  _const_spec((H_A, window)), _const_spec((H_A, 1))],
        out_specs=[pl.BlockSpec((None, 1, D_ATT), lambda b: (b, 0, 0)),
                   pl.BlockSpec((None, 1, D_ATT), lambda b: (b, 0, 0)),
                   pl.BlockSpec(blk, lambda b: (b, 0, 0))],
        out_shape=[jax.ShapeDtypeStruct((db, 1, D_ATT), _F32), jax.ShapeDtypeStruct((db, 1, D_ATT), _F32),
                   jax.ShapeDtypeStruct((db, 2 * D_ATT, window), _F32)],
        compiler_params=_params(1),
        name=f"dsw_sample_g{g}",
    )(qkv_s.reshape(db, 1, width), kvt_s, cache_t, bias_pos, b0)
    return o.reshape(db, D_ATT), lse.reshape(db, D_ATT), new


def _ffn_gate(ug, ug1, ug2, uv, uv1, uv2, cw_ref, cb_ref, cg, cv, chunk):
    def conv(u, u1, u2, c0):
        cs = slice(c0, c0 + chunk)
        return cb_ref[:, cs] + cw_ref[0:1, cs] * u2 + cw_ref[1:2, cs] * u1 + cw_ref[2:3, cs] * u
    yg = conv(ug, ug1, ug2, cg)
    yv = conv(uv, uv1, uv2, cv)
    return (yg * (1.0 / (1.0 + jnp.exp(-yg))) * yv).astype(_BF)


def _ffn_prompt_kernel(*refs, ts, d_ff, chunk, final_norm, n_mix):
    x_ref, mix_refs, refs = refs[0], refs[1:1 + n_mix], refs[1 + n_mix:]
    if final_norm:
        st_ref, g_ref, wup_ref, cw_ref, cb_ref, wdn_ref, gf_ref, o_ref, ns_ref, carry_ref, act_ref = refs
    else:
        st_ref, g_ref, wup_ref, cw_ref, cb_ref, wdn_ref, o_ref, ns_ref, carry_ref, act_ref = refs
        gf_ref = None
    s = pl.program_id(1)

    @pl.when(s == 0)
    def _():
        carry_ref[...] = st_ref[...]

    if n_mix == 7:
        o_refs, l_refs, wo_ref = mix_refs[0:3], mix_refs[3:6], mix_refs[6]
        slabs = [_merge_groups([r[p] for r in o_refs], [r[p] for r in l_refs]).astype(_BF)
                 for p in range(_PAIRS)]
        mixed = jnp.concatenate(slabs, axis=1)
    else:
        mixed, wo_ref = mix_refs[0][...], mix_refs[1]
    x = x_ref[...] + jnp.dot(mixed, wo_ref[...], preferred_element_type=_F32)
    hb = _rms(x, g_ref[...]).astype(_BF)
    row = lax.broadcasted_iota(jnp.int32, (8, chunk), 0)
    first, second = row == 0, row == 1

    def up(c0):
        u = jnp.dot(hb, wup_ref[:, c0:c0 + chunk], preferred_element_type=_F32)
        p2 = carry_ref[0:1, c0:c0 + chunk]
        p1 = carry_ref[1:2, c0:c0 + chunk]
        r1, r2 = pltpu.roll(u, 1, 0), pltpu.roll(u, 2, 0)
        u1 = jnp.concatenate([jnp.where(first, p1, r1[:8]), r1[8:]], axis=0)
        u2 = jnp.concatenate([jnp.where(first, p2, jnp.where(second, p1, r2[:8])), r2[8:]], axis=0)
        carry_ref[:, c0:c0 + chunk] = u[ts - 2:ts, :]
        return u, u1, u2

    for c in range(d_ff // chunk):
        cg, cv = c * chunk, d_ff + c * chunk
        act_ref[:, cg:cg + chunk] = _ffn_gate(*up(cg), *up(cv), cw_ref, cb_ref, cg, cv, chunk)
    y = x + jnp.dot(act_ref[...], wdn_ref[...], preferred_element_type=_F32)
    if final_norm:
        y = _rms(y, gf_ref[...])
    o_ref[...] = y

    @pl.when(s == pl.num_programs(1) - 1)
    def _():
        ns_ref[...] = carry_ref[...]


def _ffn_prompt(x3d, mix, w_o, state, g, w_up, conv_w, conv_b, w_down, g_final=None, ts=512, chunk=256):
    batch, seq, d = x3d.shape
    d_ff = w_down.shape[0]
    assert d_ff % chunk == 0 and seq % ts == 0
    final_norm = g_final is not None
    tile = pl.BlockSpec((None, ts, d), lambda b, s: (b, s, 0))
    if isinstance(mix, tuple):
        slab = pl.BlockSpec((None, _PAIRS, ts, LANES), lambda b, s: (b, 0, s, 0))
        mix_args, mix_specs = [*mix[0], *mix[1]], [slab] * 6
    else:
        mix_args, mix_specs = [mix], [pl.BlockSpec((None, ts, mix.shape[-1]), lambda b, s: (b, s, 0))]
    mix_args.append(w_o)
    mix_specs.append(_const_spec(w_o.shape))
    args = [x3d, *mix_args, state, g.reshape(1, d), w_up, conv_w, conv_b.reshape(1, 2 * d_ff), w_down]
    specs = [tile, *mix_specs,
             pl.BlockSpec((None, CONV_W - 1, 2 * d_ff), lambda b, s: (b, 0, 0)),
             _const_spec((1, d)), _const_spec((d, 2 * d_ff)), _const_spec((CONV_W, 2 * d_ff)),
             _const_spec((1, 2 * d_ff)), _const_spec((d_ff, d))]
    if final_norm:
        args.append(g_final.reshape(1, d))
        specs.append(_const_spec((1, d)))
    return pl.pallas_call(
        functools.partial(_ffn_prompt_kernel, ts=ts, d_ff=d_ff, chunk=chunk, final_norm=final_norm,
                          n_mix=len(mix_args)),
        grid=(batch, seq // ts),
        in_specs=specs,
        out_specs=[pl.BlockSpec((None, ts, d), lambda b, s: (b, s, 0)),
                   pl.BlockSpec((None, CONV_W - 1, 2 * d_ff), lambda b, s: (b, 0, 0))],
        out_shape=[jax.ShapeDtypeStruct((batch, seq, d), _F32),
                   jax.ShapeDtypeStruct((batch, CONV_W - 1, 2 * d_ff), _F32)],
        scratch_shapes=[pltpu.VMEM((CONV_W - 1, 2 * d_ff), _F32), pltpu.VMEM((ts, d_ff), _BF)],
        compiler_params=_params(2),
        name="conv_ffn_prompt",
    )(*args)


def _ffn_sample_kernel(*refs, d_ff, chunk, final_norm):
    if final_norm:
        x_ref, p2_ref, p1_ref, g_ref, wup_ref, cw_ref, cb_ref, wdn_ref, gf_ref, o_ref, u_ref, act_ref = refs
    else:
        x_ref, p2_ref, p1_ref, g_ref, wup_ref, cw_ref, cb_ref, wdn_ref, o_ref, u_ref, act_ref = refs
        gf_ref = None
    x = x_ref[...]
    hb = _rms(x, g_ref[...]).astype(_BF)

    def up(c0):
        cs = slice(c0, c0 + chunk)
        u = jnp.dot(hb, wup_ref[:, cs], preferred_element_type=_F32)
        u_ref[:, cs] = u
        return u, p1_ref[:, cs], p2_ref[:, cs]

    for c in range(d_ff // chunk):
        cg, cv = c * chunk, d_ff + c * chunk
        act_ref[:, cg:cg + chunk] = _ffn_gate(*up(cg), *up(cv), cw_ref, cb_ref, cg, cv, chunk)
    y = x + jnp.dot(act_ref[...], wdn_ref[...], preferred_element_type=_F32)
    if final_norm:
        y = _rms(y, gf_ref[...])
    o_ref[...] = y


def _ffn_sample(x2d, prev2, prev1, g, w_up, conv_w, conv_b, w_down, g_final=None, chunk=256):
    db, d = x2d.shape
    d_ff = w_down.shape[0]
    final_norm = g_final is not None
    args = [x2d, prev2, prev1, g.reshape(1, d), w_up, conv_w, conv_b.reshape(1, 2 * d_ff), w_down]
    specs = [_const_spec((db, d)), _const_spec((db, 2 * d_ff)), _const_spec((db, 2 * d_ff)),
             _const_spec((1, d)), _const_spec((d, 2 * d_ff)), _const_spec((CONV_W, 2 * d_ff)),
             _const_spec((1, 2 * d_ff)), _const_spec((d_ff, d))]
    if final_norm:
        args.append(g_final.reshape(1, d))
        specs.append(_const_spec((1, d)))
    return pl.pallas_call(
        functools.partial(_ffn_sample_kernel, d_ff=d_ff, chunk=chunk, final_norm=final_norm),
        grid=(1,),
        in_specs=specs,
        out_specs=[pl.BlockSpec((db, d), lambda i: (0, 0)), pl.BlockSpec((db, 2 * d_ff), lambda i: (0, 0))],
        out_shape=[jax.ShapeDtypeStruct((db, d), _F32), jax.ShapeDtypeStruct((db, 2 * d_ff), _F32)],
        scratch_shapes=[pltpu.VMEM((db, d_ff), _BF)],
        compiler_params=_params(1),
        name="conv_ffn_sample",
    )(*args)


def _mla_proj_kernel(x_ref, ck_ref, sk_ref, g_ref, win_ref, gq_ref, gkv_ref, wq_ref, wk_ref, wv_ref,
                     q_ref, k_ref, v_ref, ckv_ref, kr_ref):
    hb = _rms(x_ref[...], g_ref[...]).astype(_BF)
    proj = jnp.dot(hb, win_ref[...], preferred_element_type=_F32)
    c_q = _rms(proj[:, :Q_LORA], gq_ref[...])
    c_kv = _rms(proj[:, Q_LORA:Q_LORA + KV_LORA], gkv_ref[...])
    ckv_ref[...] = c_kv
    off = Q_LORA + KV_LORA
    ck, sk = ck_ref[...], sk_ref[...]
    kr_pad = proj[:, off:off + LANES] * ck + proj[:, off + LANES:off + 2 * LANES] * sk
    kr_ref[...] = kr_pad[:, QK_NOPE:QK_NOPE + QK_ROPE]
    scale = (QK_NOPE + QK_ROPE) ** -0.5 * LOG2E
    lane = lax.broadcasted_iota(jnp.int32, ck.shape, 1)
    nope = (lane < QK_NOPE).astype(_F32)
    ones_col = (lane == V_HEAD).astype(_F32)
    cq_tab = scale * (ck + nope)
    sq_tab = scale * sk
    cqb = c_q.astype(_BF)
    ckvb = c_kv.astype(_BF)
    hw = H_B * LANES
    for hp in range(H_B // 2):
        sl = slice(2 * hp * LANES, 2 * (hp + 1) * LANES)
        qa = jnp.dot(cqb, wq_ref[:, sl], preferred_element_type=_F32)
        qs = jnp.dot(cqb, wq_ref[:, hw + 2 * hp * LANES:hw + 2 * (hp + 1) * LANES], preferred_element_type=_F32)
        kn = jnp.dot(ckvb, wk_ref[:, sl], preferred_element_type=_F32)
        vv = jnp.dot(ckvb, wv_ref[:, sl], preferred_element_type=_F32)
        for hh in range(2):
            h, one = 2 * hp + hh, slice(hh * LANES, (hh + 1) * LANES)
            q_ref[h] = (qa[:, one] * cq_tab + qs[:, one] * sq_tab).astype(_BF)
            k_ref[h] = (kn[:, one] + kr_pad).astype(_BF)
            v_ref[h] = (vv[:, one] + ones_col).astype(_BF)


def _mla_proj(x2d, ck_tab, sk_tab, g, w_in_ext, g_q, g_kv, w_q_both, w_k_pad, w_v, n_seq, tm):
    m, d = x2d.shape
    rows = m // n_seq
    tm = min(tm, rows)
    tps = rows // tm
    x_map = lambda s, b: (b * tps + s, 0)
    t_map = lambda s, b: (s, 0)
    return pl.pallas_call(
        _mla_proj_kernel,
        grid=(tps, n_seq),
        in_specs=[pl.BlockSpec((tm, d), x_map), pl.BlockSpec((tm, LANES), t_map), pl.BlockSpec((tm, LANES), t_map),
                  _const_spec((1, d)), _const_spec(w_in_ext.shape), _const_spec((1, Q_LORA)),
                  _const_spec((1, KV_LORA)), _const_spec(w_q_both.shape), _const_spec(w_k_pad.shape),
                  _const_spec(w_v.shape)],
        out_specs=[pl.BlockSpec((H_B, tm, LANES), lambda s, b: (0, b * tps + s, 0)),
                   pl.BlockSpec((H_B, tm, LANES), lambda s, b: (0, b * tps + s, 0)),
                   pl.BlockSpec((H_B, tm, LANES), lambda s, b: (0, b * tps + s, 0)),
                   pl.BlockSpec((tm, KV_LORA), x_map), pl.BlockSpec((tm, QK_ROPE), x_map)],
        out_shape=[jax.ShapeDtypeStruct((H_B, m, LANES), _BF), jax.ShapeDtypeStruct((H_B, m, LANES), _BF),
                   jax.ShapeDtypeStruct((H_B, m, LANES), _BF),
                   jax.ShapeDtypeStruct((m, KV_LORA), _F32), jax.ShapeDtypeStruct((m, QK_ROPE), _F32)],
        compiler_params=_params(2),
        name="mla_proj",
    )(x2d, ck_tab, sk_tab, g.reshape(1, d), w_in_ext, g_q.reshape(1, Q_LORA), g_kv.reshape(1, KV_LORA),
      w_q_both, w_k_pad, w_v)


def _mla_prompt_kernel(q_ref, k_ref, v_ref, o_ref, s_ref, *, seq, tq, ck):
    lo = lax.broadcasted_iota(jnp.int32, (tq, LANES), 1) < V_HEAD
    row = lax.broadcasted_iota(jnp.int32, (tq, tq), 0)
    colk = lax.broadcasted_iota(jnp.int32, (tq, tq), 1)
    causal = colk <= row
    units = [(qi, hh) for qi in range(seq // tq) for hh in range(2)]
    state = [dict(mx=None, m=None, acc=None) for _ in units]
    outs = {}

    def chunks_of(qi):
        kv_len = (qi + 1) * tq
        return [(c0, min(ck, kv_len - c0)) for c0 in range(0, kv_len, ck)]

    def score_tasks(u):
        (qi, hh), st, slot = units[u], state[u], u % 2
        q0, kv_len = qi * tq, (qi + 1) * tq

        def task(c0, cw):
            s = lax.dot_general(q_ref[hh, q0:q0 + tq, :], k_ref[hh, c0:c0 + cw, :], _NT,
                                preferred_element_type=_F32)
            if c0 + cw == kv_len:
                sd = jnp.where(causal, s[:, cw - tq:], NEG)
                if cw > tq:
                    s_ref[slot, :, c0:c0 + cw - tq] = s[:, :cw - tq]
                s_ref[slot, :, kv_len - tq:kv_len] = sd
                tiles = [s[:, j:j + LANES] for j in range(0, cw - tq, LANES)]
                tiles += [sd[:, j:j + LANES] for j in range(0, tq, LANES)]
            else:
                s_ref[slot, :, c0:c0 + cw] = s
                tiles = [s[:, j:j + LANES] for j in range(0, cw, LANES)]
            for t in tiles:
                st["mx"] = t if st["mx"] is None else jnp.maximum(st["mx"], t)

        return [functools.partial(task, c0, cw) for c0, cw in chunks_of(qi)]

    def value_tasks(u):
        (qi, hh), st, slot = units[u], state[u], u % 2
        q0 = qi * tq

        def task(c0, cw, first, last):
            if first:
                st["m"] = jnp.max(st["mx"], axis=-1, keepdims=True)
            p = jnp.exp2(s_ref[slot, :, c0:c0 + cw] - st["m"]).astype(_BF)
            pv = jnp.dot(p, v_ref[hh, c0:c0 + cw, :], preferred_element_type=_F32)
            st["acc"] = pv if st["acc"] is None else st["acc"] + pv
            if last:
                outs[hh] = st["acc"] / st["acc"][:, V_HEAD:V_HEAD + 1]
                if hh == 1:
                    o_ref[q0:q0 + tq, :] = jnp.where(lo, outs[0], pltpu.roll(outs[1], V_HEAD, 1)).astype(o_ref.dtype)

        ch = chunks_of(qi)
        return [functools.partial(task, c0, cw, i == 0, i == len(ch) - 1) for i, (c0, cw) in enumerate(ch)]

    pending = []
    for u in range(len(units) + 1):
        scoring = score_tasks(u) if u < len(units) else []
        for i in range(max(len(scoring), len(pending))):
            if i < len(scoring):
                scoring[i]()
            if i < len(pending):
                pending[i]()
        pending = value_tasks(u) if u < len(units) else []


def _mla_prompt_attn(q, k, v, batch, seq, tq=256, ck=512):
    m = batch * seq
    blk = pl.BlockSpec((2, seq, LANES), lambda b, hp: (hp, b, 0))
    return pl.pallas_call(
        functools.partial(_mla_prompt_kernel, seq=seq, tq=tq, ck=ck),
        grid=(batch, H_B // 2),
        in_specs=[blk, blk, blk],
        out_specs=pl.BlockSpec((seq, LANES), lambda b, hp: (b, hp)),
        out_shape=jax.ShapeDtypeStruct((m, H_B * V_HEAD), _BF),
        scratch_shapes=[pltpu.VMEM((2, tq, seq), _F32)],
        compiler_params=_params(2),
        name="mla_prompt_attn",
    )(q, k, v)


def _mla_absorb_kernel(q_ref, wuk_ref, o_ref):
    for h in range(H_B):
        o_ref[h] = jnp.dot(q_ref[h], wuk_ref[h], preferred_element_type=_F32).astype(o_ref.dtype)


def _mla_absorb(q, w_uk_t_pad):
    db = q.shape[1]
    return pl.pallas_call(
        _mla_absorb_kernel,
        grid=(1,),
        in_specs=[_const_spec(q.shape), _const_spec(w_uk_t_pad.shape)],
        out_specs=pl.BlockSpec((H_B, db, KV_LORA), lambda i: (0, 0, 0)),
        out_shape=jax.ShapeDtypeStruct((H_B, db, KV_LORA), _BF),
        compiler_params=_params(1),
        name="mla_absorb_q",
    )(q, w_uk_t_pad)


def _mla_sample_kernel(pt_ref, qlat_ref, qrope_ref, cnew_ref, krnew_ref, ckv_hbm, kr_hbm, o_ref,
                       ckv_buf, kr_buf, sems, *, pages, n_pages):
    b = pl.program_id(0)
    n_chunks = n_pages // pages

    def copies(page_of, slot):
        out = []
        for t in range(pages):
            page = page_of(t)
            out.append(pltpu.make_async_copy(ckv_hbm.at[page], ckv_buf.at[slot, t], sems.at[0, slot]))
            out.append(pltpu.make_async_copy(kr_hbm.at[page], kr_buf.at[slot, t], sems.at[1, slot]))
        return out

    def start(seq, chunk, slot):
        for cp in copies(lambda t: pt_ref[seq * n_pages + chunk * pages + t], slot):
            cp.start()

    def wait(slot):
        for cp in copies(lambda t: 0, slot):
            cp.wait()

    @pl.when(b == 0)
    def _():
        start(0, 0, 0)
        start(0, 1, 1)

    qlat = qlat_ref[...]
    qrope = qrope_ref[...]

    def update(carry, s, weighted_values):
        m_prev, l_prev, acc = carry
        m_new = jnp.maximum(m_prev, jnp.max(s, axis=-1, keepdims=True))
        alpha = jnp.exp2(m_prev - m_new)
        p = jnp.exp2(s - m_new)
        return (m_new, alpha * l_prev + jnp.sum(p, axis=-1, keepdims=True),
                alpha * acc + weighted_values(p.astype(_BF)))

    carry = (jnp.full((H_B, 1), -jnp.inf, _F32), jnp.zeros((H_B, 1), _F32), jnp.zeros((H_B, KV_LORA), _F32))
    prev = None
    for c in range(n_chunks):
        ahead = c + 2
        if ahead < n_chunks:
            start(b, ahead, ahead)
        else:
            @pl.when(b + 1 < pl.num_programs(0))
            def _(ahead=ahead):
                start(b + 1, ahead - n_chunks, ahead - n_chunks)
        wait(c)
        ckv = ckv_buf[c].reshape(pages * ckv_buf.shape[2], KV_LORA).astype(_BF)
        s_rope = jnp.concatenate(
            [jnp.dot(qrope, kr_buf[c, t].astype(_BF), preferred_element_type=_F32) for t in range(pages)],
            axis=1)
        s = lax.dot_general(qlat, ckv, _NT, preferred_element_type=_F32) + s_rope
        if prev is not None:
            carry = update(carry, prev[0], lambda pb, kv=prev[1]: jnp.dot(pb, kv, preferred_element_type=_F32))
        prev = (s, ckv)
    carry = update(carry, prev[0], lambda pb, kv=prev[1]: jnp.dot(pb, kv, preferred_element_type=_F32))

    cn = cnew_ref[...].astype(_BF).astype(_F32)
    krn = krnew_ref[...].astype(_BF).astype(_F32)
    s = jnp.sum(qlat.astype(_F32) * cn, axis=-1, keepdims=True) \
        + jnp.sum(qrope.astype(_F32) * krn, axis=-1, keepdims=True)
    _, l, acc = update(carry, s, lambda pb: pb.astype(_F32) * cn)
    o_ref[...] = (acc / l).astype(o_ref.dtype)


def _mla_sample_attn(page_table, qlat, qrope, c_new, kr_new, ckv_pool, kr_pool_t, pages=32):
    db, n_pages = page_table.shape
    page = ckv_pool.shape[1]
    pages = min(pages, n_pages // 4)
    assert n_pages % pages == 0 and n_pages // pages >= 4
    per_b = lambda b, pt: (b, 0, 0)
    grid_spec = pltpu.PrefetchScalarGridSpec(
        num_scalar_prefetch=1,
        grid=(db,),
        in_specs=[pl.BlockSpec((None, H_B, KV_LORA), per_b), pl.BlockSpec((None, H_B, QK_ROPE), per_b),
                  pl.BlockSpec((None, 1, KV_LORA), per_b), pl.BlockSpec((None, 1, QK_ROPE), per_b),
                  pl.BlockSpec(memory_space=pl.ANY), pl.BlockSpec(memory_space=pl.ANY)],
        out_specs=pl.BlockSpec((None, H_B, KV_LORA), per_b),
        scratch_shapes=[pltpu.VMEM((n_pages // pages, pages, page, KV_LORA), _F32),
                        pltpu.VMEM((n_pages // pages, pages, QK_ROPE, page), _F32),
                        pltpu.SemaphoreType.DMA((2, n_pages // pages))],
    )
    return pl.pallas_call(
        functools.partial(_mla_sample_kernel, pages=pages, n_pages=n_pages),
        grid_spec=grid_spec,
        out_shape=jax.ShapeDtypeStruct((db, H_B, KV_LORA), _BF),
        compiler_params=_params(1),
        name="mla_sample_attn",
    )(page_table.reshape(-1), qlat, qrope, c_new, kr_new, ckv_pool, kr_pool_t)


def _mla_unabsorb_kernel(o_ref, wuv_ref, out_ref):
    for p in range(H_B // 2):
        y = jnp.dot(o_ref[2 * p], wuv_ref[2 * p], preferred_element_type=_F32) \
            + jnp.dot(o_ref[2 * p + 1], wuv_ref[2 * p + 1], preferred_element_type=_F32)
        out_ref[:, p * LANES:(p + 1) * LANES] = y.astype(out_ref.dtype)


def _mla_unabsorb(o_lat_hm, w_uv_pair):
    db = o_lat_hm.shape[1]
    return pl.pallas_call(
        _mla_unabsorb_kernel,
        grid=(1,),
        in_specs=[_const_spec(o_lat_hm.shape), _const_spec(w_uv_pair.shape)],
        out_specs=pl.BlockSpec((db, H_B * V_HEAD), lambda i: (0, 0)),
        out_shape=jax.ShapeDtypeStruct((db, H_B * V_HEAD), _BF),
        compiler_params=_params(1),
        name="mla_unabsorb_o",
    )(o_lat_hm, w_uv_pair)


def _t5_bucket(dist):
    n = np.asarray(dist)
    max_exact = N_BUCKETS // 2
    large = max_exact + (np.log(np.maximum(n, 1) / max_exact) / np.log(T5_MAX_DISTANCE / max_exact)
                         * (N_BUCKETS - max_exact)).astype(np.int32)
    large = np.minimum(large, N_BUCKETS - 1)
    return np.where(n < max_exact, n, large).astype(np.int32)


def _dsw_biases(table):
    out = []
    for g, (w, d) in enumerate(zip(WINDOWS, DILATIONS)):
        buckets = _t5_bucket(np.arange(w // d + 1) * d)
        out.append(table[buckets][:, g * H_A:(g + 1) * H_A].T.astype(_F32))
    return out


def _dsw_bias_blocks(bias_g):
    n = 3 * BLK
    diag = bias_g[:, np.clip(2 * BLK - 1 - np.arange(n), 0, BLK)]
    skew = jnp.tile(diag, (1, BLK))[:, :BLK * (n - 1)].reshape(bias_g.shape[0], BLK, n - 1)
    return skew[:, :, BLK - 1:3 * BLK - 1]


def _rope_pad_tables(pos):
    inv = jnp.asarray(ROPE_BASE ** (-np.arange(0, QK_ROPE, 2) / QK_ROPE), dtype=_F32)
    ang = pos.astype(_F32)[:, None] * inv[None, :]
    cos, sin = jnp.cos(ang), jnp.sin(ang)
    n = pos.shape[0]
    zl = jnp.zeros((n, QK_NOPE), _F32)
    zr = jnp.zeros((n, LANES - QK_NOPE - QK_ROPE), _F32)
    return (jnp.concatenate([zl, cos, cos, zr], axis=1), jnp.concatenate([zl, -sin, sin, zr], axis=1))


def _mla_weights(w_in, w_q, w_kv):
    d = w_in.shape[0]
    half = QK_ROPE // 2
    off = Q_LORA + KV_LORA
    kr = w_in[:, off:]
    kr_sw = jnp.concatenate([kr[:, half:], kr[:, :half]], axis=1)
    zl = jnp.zeros((d, QK_NOPE), w_in.dtype)
    zr = jnp.zeros((d, LANES - QK_NOPE - QK_ROPE), w_in.dtype)
    w_in_ext = jnp.concatenate([w_in[:, :off], zl, kr, zr, zl, kr_sw, zr], axis=1).astype(_BF)

    dh = QK_NOPE + QK_ROPE
    wq3 = w_q.reshape(Q_LORA, H_B, dh)
    pad = jnp.zeros((Q_LORA, H_B, LANES - dh), w_q.dtype)
    wq_pad = jnp.concatenate([wq3, pad], axis=2)
    wq_sw = jnp.concatenate([jnp.zeros((Q_LORA, H_B, QK_NOPE), w_q.dtype), wq3[:, :, QK_NOPE + half:],
                             wq3[:, :, QK_NOPE:QK_NOPE + half], pad], axis=2)
    w_q_both = jnp.concatenate([wq_pad.reshape(Q_LORA, H_B * LANES), wq_sw.reshape(Q_LORA, H_B * LANES)],
                               axis=1).astype(_BF)

    wkv3 = w_kv.reshape(KV_LORA, H_B, QK_NOPE + V_HEAD)
    w_uk, w_uv = wkv3[:, :, :QK_NOPE], wkv3[:, :, QK_NOPE:]
    w_k_pad = jnp.concatenate([w_uk, jnp.zeros((KV_LORA, H_B, LANES - QK_NOPE), w_kv.dtype)], axis=2)
    w_k_pad = w_k_pad.reshape(KV_LORA, H_B * LANES).astype(_BF)
    w_v = jnp.concatenate([w_uv, jnp.zeros((KV_LORA, H_B, LANES - V_HEAD), w_kv.dtype)], axis=2)
    w_v = w_v.reshape(KV_LORA, H_B * LANES).astype(_BF)
    w_uk_t = jnp.transpose(w_uk, (1, 2, 0))
    w_uk_t_pad = jnp.concatenate([w_uk_t, jnp.zeros((H_B, LANES - QK_NOPE, KV_LORA), w_kv.dtype)], axis=1)
    w_uv_h = jnp.transpose(w_uv, (1, 0, 2))
    zero = jnp.zeros_like(w_uv_h)
    even = jnp.concatenate([w_uv_h, zero], axis=2)
    odd = jnp.concatenate([zero, w_uv_h], axis=2)
    is_even = (jnp.arange(H_B) % 2 == 0)[:, None, None]
    w_uv_pair = jnp.where(is_even, even, odd)
    return w_in_ext, w_q_both, w_k_pad, w_v, w_uk_t_pad.astype(_BF), w_uv_pair.astype(_BF)


def kernel(x_prompt, x_sample, cache_dsw_g0, cache_dsw_g1, cache_dsw_g2, cache_mla_ckv, cache_mla_kr,
           state_ffn_conv, page_table, rel_bias_table, norm_mix, norm_ffn, norm_final, w_qkv_dsw, w_o_dsw,
           w_in_mla, g_q_mla, g_kv_mla, w_q_mla, w_kv_mla, w_o_mla, w_up_ffn, conv_w_ffn, conv_b_ffn, w_down_ffn):
    batch, seq, d = x_prompt.shape
    db, t_new, _ = x_sample.shape
    assert t_new == 1
    depth = norm_mix.shape[0]
    d_ff = w_down_ffn.shape[1]
    past_len = page_table.shape[1] * cache_mla_ckv.shape[2]
    m = batch * seq
    dsw_caches = (cache_dsw_g0, cache_dsw_g1, cache_dsw_g2)
    biases = _dsw_biases(rel_bias_table)
    bias_blocks = [_dsw_bias_blocks(bg) for bg in biases]

    xp = x_prompt.reshape(m, d)
    xs = x_sample.reshape(db, d)
    dsw_new_p = [[] for _ in range(N_GROUPS)]
    dsw_new_s = [[] for _ in range(N_GROUPS)]
    ckv_p, ckv_s, kr_p, kr_s, conv_p, conv_s = [], [], [], [], [], []
    zero_state = jnp.zeros((batch, CONV_W - 1, 2 * d_ff), _F32)

    for i in range(depth):
        if i % 2 == 0:
            a = i // 2
            w_qkv = w_qkv_dsw[a].astype(_BF)
            w_o = w_o_dsw[a].astype(_BF)
            *qkv_groups, c0, c1, c2 = _qkv_prompt(xp.reshape(batch, seq, d), norm_mix[i], w_qkv)
            for g, c in enumerate((c0, c1, c2)):
                c = c.reshape(batch, 2, H_A, DH_A, min(WINDOWS[g], seq))
                dsw_new_p[g].append(jnp.transpose(c, (0, 4, 1, 2, 3)))
            mix_p = tuple(zip(*[_dsw_prompt_attn(qkv_groups[g], bias_blocks[g], g) for g in range(N_GROUPS)]))
            w_o_p = w_o
            qkv_s = _linear(xs, w_qkv, norm_g=norm_mix[i], name="dsw_qkv_sample")
            w_kv_t = jnp.transpose(w_qkv.reshape(d, N_GROUPS, 3, D_ATT)[:, :, 1:], (1, 2, 3, 0))
            kvt_s = _linear_t(xs, norm_mix[i], w_kv_t.reshape(N_GROUPS * 2 * D_ATT, d), "dsw_kv_sample_t")
            kvt_s = kvt_s.reshape(N_GROUPS, 2 * D_ATT, db)
            outs, lses = [], []
            for g in range(N_GROUPS):
                cache_t = jnp.transpose(dsw_caches[g][a], (0, 2, 3, 4, 1)).reshape(db, 2 * D_ATT, WINDOWS[g])
                o, lse, new = _dsw_sample(qkv_s, kvt_s, cache_t, biases[g], g)
                outs.append(o)
                lses.append(lse)
                new = new.reshape(db, 2, H_A, DH_A, WINDOWS[g])
                dsw_new_s[g].append(jnp.transpose(new, (0, 4, 1, 2, 3)))
            xs = _dsw_merge(outs, lses, w_o, xs)
        else:
            b = i // 2
            w_in_ext, w_q_both, w_k_pad, w_v, w_uk_t_pad, w_uv_pair = _mla_weights(
                w_in_mla[b], w_q_mla[b], w_kv_mla[b])
            w_o = w_o_mla[b].astype(_BF)
            ck_tab, sk_tab = _rope_pad_tables(jnp.arange(seq))
            q, k, v, ckv, kr = _mla_proj(xp, ck_tab, sk_tab, norm_mix[i], w_in_ext, g_q_mla[b], g_kv_mla[b],
                                         w_q_both, w_k_pad, w_v, n_seq=batch, tm=512)
            ckv_p.append(ckv.reshape(batch, seq, KV_LORA))
            kr_p.append(kr.reshape(batch, seq, QK_ROPE))
            mix_p = _mla_prompt_attn(q, k, v, batch, seq).reshape(batch, seq, H_B * V_HEAD)
            w_o_p = w_o
            ck_s, sk_s = _rope_pad_tables(jnp.full((db,), past_len, jnp.int32))
            q, _, _, ckv, kr = _mla_proj(xs, ck_s, sk_s, norm_mix[i], w_in_ext, g_q_mla[b], g_kv_mla[b],
                                         w_q_both, w_k_pad, w_v, n_seq=1, tm=db)
            ckv_s.append(ckv.reshape(db, 1, KV_LORA))
            kr_s.append(kr.reshape(db, 1, QK_ROPE))
            qlat = jnp.transpose(_mla_absorb(q, w_uk_t_pad), (1, 0, 2))
            qrope = jnp.transpose(q[:, :, QK_NOPE:QK_NOPE + QK_ROPE], (1, 0, 2))
            o_lat = _mla_sample_attn(page_table, qlat, qrope, ckv.reshape(db, 1, KV_LORA),
                                     kr.reshape(db, 1, QK_ROPE), cache_mla_ckv[b],
                                     jnp.transpose(cache_mla_kr[b], (0, 2, 1)))
            o = _mla_unabsorb(jnp.transpose(o_lat, (1, 0, 2)), w_uv_pair)
            xs = _linear(o, w_o, residual=xs, name="mla_out_sample")

        g_final = norm_final if i == depth - 1 else None
        w_up = w_up_ffn[i].astype(_BF)
        w_dn = w_down_ffn[i].astype(_BF)
        yp, cp = _ffn_prompt(xp.reshape(batch, seq, d), mix_p, w_o_p, zero_state, norm_ffn[i], w_up,
                             conv_w_ffn[i], conv_b_ffn[i], w_dn, g_final)
        xp = yp.reshape(m, d)
        conv_p.append(cp)
        prev2, prev1 = state_ffn_conv[i, :, 0], state_ffn_conv[i, :, 1]
        xs, u_s = _ffn_sample(xs, prev2, prev1, norm_ffn[i], w_up, conv_w_ffn[i], conv_b_ffn[i], w_dn, g_final)
        conv_s.append(jnp.stack([prev1, u_s], axis=1))

    def stack(per_layer):
        return per_layer[0][None] if len(per_layer) == 1 else jnp.stack(per_layer)

    y_prompt = xp.reshape(batch, seq, d)
    y_sample = xs.reshape(db, 1, d)
    return (y_prompt, y_sample,
            stack(dsw_new_p[0]), stack(dsw_new_s[0]), stack(dsw_new_p[1]), stack(dsw_new_s[1]),
            stack(dsw_new_p[2]), stack(dsw_new_s[2]),
            stack(ckv_p), stack(ckv_s), stack(kr_p), stack(kr_s), stack(conv_p), stack(conv_s))
```
